```python
import math
import jax, jax.numpy as jnp
from jax import lax
import numpy as np

D_MODEL = 1024
BATCH = 8
SEQ = 8192
DEPTH = 2

RMS_EPS = 1e-6
N_BRANCH = 3

S5_WIDTH = D_MODEL // 2
S5_GROUP = 16
S5_GROUPS = S5_WIDTH // S5_GROUP
S5_STATE = 64
S5_STEP_MIN = 1e-3
S5_STEP_MAX = 1e-1

ATT_HEAD_DIM = 64
ATT_PAIRS = ((128, 1), (512, 4), (2048, 16))
ATT_HEADS_PER_GROUP = 4
ATT_HEADS = len(ATT_PAIRS) * ATT_HEADS_PER_GROUP
ATT_WIDTH = ATT_HEADS * ATT_HEAD_DIM
ATT_OUT_WIDTH = ATT_HEADS_PER_GROUP * ATT_HEAD_DIM
ATT_BLOCK = 128

SSD_HEAD_DIM = 64
SSD_WIDTH = 3 * D_MODEL // 4
SSD_HEADS = SSD_WIDTH // SSD_HEAD_DIM
SSD_GROUPS = 2
SSD_STATE = 128
SSD_CONV = 4
SSD_CHUNK = 128
SSD_CONV_DIM = SSD_WIDTH + 2 * SSD_GROUPS * SSD_STATE
SSD_DT_MIN = 1e-3
SSD_DT_MAX = 1e-1

IN_SPLITS = (S5_WIDTH, S5_WIDTH,
             ATT_WIDTH, ATT_WIDTH, ATT_WIDTH, ATT_OUT_WIDTH,
             SSD_CONV_DIM, SSD_HEADS, SSD_WIDTH,
             N_BRANCH * D_MODEL)
IN_WIDTH = sum(IN_SPLITS)

kernel_name = "hybrid_s5_dilated_attn_ssd_gated_merge"


def rms_norm(x, w):
    xf = x.astype(jnp.float32)
    y = xf * lax.rsqrt(jnp.mean(xf * xf, axis=-1, keepdims=True) + RMS_EPS)
    return y * w.astype(jnp.float32)


def s5_mixer(u, a_re, a_im, log_step, b_re, b_im, c_re, c_im, d, glu_w, glu_b):
    f32 = jnp.float32
    a_re, a_im = a_re.astype(f32), a_im.astype(f32)
    b_re, b_im = b_re.astype(f32), b_im.astype(f32)
    c_re, c_im = c_re.astype(f32), c_im.astype(f32)
    Bt, S, _ = u.shape
    ug = u.reshape(Bt, S, S5_GROUPS, S5_GROUP)
    step = jnp.exp(log_step.astype(f32))[:, None]
    mag = jnp.exp(a_re * step)
    ang = a_im * step
    lam_re, lam_im = mag * jnp.cos(ang), mag * jnp.sin(ang)
    num_re, num_im = lam_re - 1.0, lam_im
    den = a_re * a_re + a_im * a_im
    f_re = (num_re * a_re + num_im * a_im) / den
    f_im = (num_im * a_re - num_re * a_im) / den
    bb_re = f_re[..., None] * b_re - f_im[..., None] * b_im
    bb_im = f_re[..., None] * b_im + f_im[..., None] * b_re
    bu_re = jnp.einsum('gpi,bsgi->bsgp', bb_re, ug)
    bu_im = jnp.einsum('gpi,bsgi->bsgp', bb_im, ug)
    lam_re_t = jnp.broadcast_to(lam_re, (1, S) + lam_re.shape)
    lam_im_t = jnp.broadcast_to(lam_im, (1, S) + lam_im.shape)

    def combine(left, right):
        ar_l, ai_l, br_l, bi_l = left
        ar_r, ai_r, br_r, bi_r = right
        return (ar_r * ar_l - ai_r * ai_l,
                ar_r * ai_l + ai_r * ar_l,
                ar_r * br_l - ai_r * bi_l + br_r,
                ar_r * bi_l + ai_r * br_l + bi_r)

    _, _, h_re, h_im = lax.associative_scan(combine, (lam_re_t, lam_im_t, bu_re, bu_im), axis=1)
    y = jnp.einsum('gip,bsgp->bsgi', c_re, h_re) - jnp.einsum('gip,bsgp->bsgi', c_im, h_im)
    y = y.reshape(Bt, S, S5_WIDTH) + d.astype(f32) * u
    g = jax.nn.gelu(y)
    return g * jax.nn.sigmoid(g @ glu_w + glu_b)


def dilated_window_attention(q, k, v, window, dilation):
    Bt, S, H, Dh = q.shape
    span = window // dilation
    seg = dilation * ATT_BLOCK
    S_pad = -(-S // seg) * seg
    L = S_pad // dilation
    nb = L // ATT_BLOCK

    def to_strided(t):
        t = jnp.pad(t, ((0, 0), (0, S_pad - S), (0, 0), (0, 0)))
        t = t.reshape(Bt, L, dilation, H, Dh).transpose(0, 2, 1, 3, 4)
        return t.reshape(Bt, dilation, nb, ATT_BLOCK, H, Dh)

    def with_prev(t):
        prev = jnp.pad(t[:, :, :-1], ((0, 0), (0, 0), (1, 0), (0, 0), (0, 0), (0, 0)))
        return jnp.concatenate([prev, t], axis=3)

    qb = to_strided(q)
    kb = with_prev(to_strided(k))
    vb = with_prev(to_strided(v))
    s = jnp.einsum('brnqhd,brnkhd->brnhqk', qb, kb) * (Dh ** -0.5)
    qi = jnp.arange(ATT_BLOCK)[:, None] + ATT_BLOCK
    kj = jnp.arange(2 * ATT_BLOCK)[None, :]
    band = (qi - kj >= 0) & (qi - kj <= span)
    has_prev = (jnp.arange(nb) > 0)[:, None, None] | (kj >= ATT_BLOCK)[None]
    mask = band[None] & has_prev
    s = jnp.where(mask[None, None, :, None], s, -jnp.inf)
    m = jnp.max(s, axis=-1, keepdims=True)
    p = jnp.exp(s - m)
    l = jnp.sum(p, axis=-1, keepdims=True)
    o = jnp.einsum('brnhqk,brnkhd->brnqhd', p / l, vb)
    lse = (m + jnp.log(l))[..., 0]
    o = o.reshape(Bt, dilation, L, H, Dh).transpose(0, 2, 1, 3, 4).reshape(Bt, S_pad, H, Dh)[:, :S]
    lse = lse.transpose(0, 1, 2, 4, 3).reshape(Bt, dilation, L, H)
    lse = lse.transpose(0, 2, 1, 3).reshape(Bt, S_pad, H)[:, :S]
    return o, lse


def attention_mixer(q, k, v, q_norm_w, k_norm_w):
    Bt, S = q.shape[:2]
    q = rms_norm(q, q_norm_w)
    k = rms_norm(k, k_norm_w)
    v = v.astype(jnp.float32)
    outs, lses = [], []
    for g, (window, dilation) in enumerate(ATT_PAIRS):
        sl = slice(g * ATT_HEADS_PER_GROUP, (g + 1) * ATT_HEADS_PER_GROUP)
        o, l = dilated_window_attention(q[:, :, sl], k[:, :, sl], v[:, :, sl], window, dilation)
        outs.append(o)
        lses.append(l)
    o = jnp.stack(outs, axis=0)
    alpha = jax.nn.softmax(jnp.stack(lses, axis=0), axis=0)
    y = jnp.sum(alpha[..., None] * o, axis=0)
    return y.reshape(Bt, S, ATT_OUT_WIDTH)


def segsum(a):
    T = a.shape[-1]
    cs = jnp.cumsum(a, axis=-1)
    diff = cs[..., :, None] - cs[..., None, :]
    return jnp.where(jnp.tril(jnp.ones((T, T), dtype=bool)), diff, -jnp.inf)


def causal_depthwise_conv(x, w, b):
    y = lax.conv_general_dilated(x, w.astype(x.dtype)[:, None, :], window_strides=(1,),
                                 padding=((SSD_CONV - 1, 0),),
                                 dimension_numbers=('NWC', 'WIO', 'NWC'),
                                 feature_group_count=x.shape[-1])
    return y + b


def ssd_mixer(xbc, dt, z, conv_w, conv_b, dt_bias, a_log, d, norm_w):
    f32 = jnp.float32
    Bt, S, _ = xbc.shape
    E = SSD_HEADS // SSD_GROUPS
    nc = S // SSD_CHUNK
    xbc = jax.nn.silu(causal_depthwise_conv(xbc, conv_w, conv_b))
    xs, bm, cm = jnp.split(xbc, [SSD_WIDTH, SSD_WIDTH + SSD_GROUPS * SSD_STATE], axis=-1)
    xs = xs.reshape(Bt, nc, SSD_CHUNK, SSD_GROUPS, E, SSD_HEAD_DIM)
    bm = bm.reshape(Bt, nc, SSD_CHUNK, SSD_GROUPS, SSD_STATE)
    cm = cm.reshape(Bt, nc, SSD_CHUNK, SSD_GROUPS, SSD_STATE)
    dt = jax.nn.softplus(dt + dt_bias.astype(f32))
    a = -jnp.exp(a_log.astype(f32))
    dt_c = dt.reshape(Bt, nc, SSD_CHUNK, SSD_GROUPS, E)
    a_dt = (dt_c * a.reshape(SSD_GROUPS, E)).transpose(0, 3, 4, 1, 2)
    xdt = xs * dt_c[..., None]
    a_cs = jnp.cumsum(a_dt, axis=-1)
    decay_in = jnp.exp(segsum(a_dt))
    cb = jnp.einsum('bclgn,bcsgn->bgcls', cm, bm)
    y_diag = jnp.einsum('bgcls,bgecls,bcsgep->bclgep', cb, decay_in, xdt)
    decay_st = jnp.exp(a_cs[..., -1:] - a_cs)
    states = jnp.einsum('bclgn,bgecl,bclgep->bcgepn', bm, decay_st, xdt)
    states = jnp.concatenate([jnp.zeros_like(states[:, :1]), states], axis=1)
    chunk_a = jnp.pad(a_cs[..., -1], ((0, 0), (0, 0), (0, 0), (1, 0)))
    decay_chunk = jnp.exp(segsum(chunk_a))
    states = jnp.einsum('bgezc,bcgepn->bzgepn', decay_chunk, states)[:, :-1]
    y_off = jnp.einsum('bclgn,bcgepn,bgecl->bclgep', cm, states, jnp.exp(a_cs))
    y = y_diag + y_off + xs * d.astype(f32).reshape(SSD_GROUPS, E)[:, :, None]
    y = y.reshape(Bt, S, SSD_WIDTH)
    return rms_norm(y * jax.nn.silu(z), norm_w)


def hybrid_layer(x, norm_w, w_in, s5_a_re, s5_a_im, s5_log_step, s5_b_re, s5_b_im, s5_c_re,
                 s5_c_im, s5_d, s5_glu_w, s5_glu_b, q_norm_w, k_norm_w, conv_w, conv_b,
                 dt_bias, ssd_a_log, ssd_d, ssd_norm_w, proj_a, proj_b, proj_c, w_out):
    Bt, S, _ = x.shape
    h = rms_norm(x, norm_w)
    proj = h @ w_in
    (u_a, z_a, q, k, v, z_b, xbc, dt, z_c, gate_logits) = jnp.split(
        proj, np.cumsum(IN_SPLITS)[:-1].tolist(), axis=-1)
    y_a = s5_mixer(u_a, s5_a_re, s5_a_im, s5_log_step, s5_b_re, s5_b_im, s5_c_re, s5_c_im,
                   s5_d, s5_glu_w, s5_glu_b) * jax.nn.silu(z_a)
    hd = (Bt, S, ATT_HEADS, ATT_HEAD_DIM)
    y_b = attention_mixer(q.reshape(hd), k.reshape(hd), v.reshape(hd),
                          q_norm_w, k_norm_w) * jax.nn.silu(z_b)
    y_c = ssd_mixer(xbc, dt, z_c, conv_w, conv_b, dt_bias, ssd_a_log, ssd_d, ssd_norm_w)
    gates = jax.nn.sigmoid(gate_logits).reshape(Bt, S, N_BRANCH, D_MODEL)
    merged = (gates[:, :, 0] * (y_a @ proj_a)
              + gates[:, :, 1] * (y_b @ proj_b)
              + gates[:, :, 2] * (y_c @ proj_c))
    return x + (merged @ w_out).astype(x.dtype)


def setup_inputs(seed: int = 0) -> dict:
    key = jax.random.key(seed)
    ks = jax.random.split(key, 32)
    L = DEPTH
    nrm = jax.random.normal
    P, I, G = S5_STATE, S5_GROUP, S5_GROUPS
    x = nrm(ks[0], (BATCH, SEQ, D_MODEL), jnp.float32)
    norm_w = 1.0 + 0.02 * nrm(ks[1], (L, D_MODEL))
    w_in = nrm(ks[2], (L, D_MODEL, IN_WIDTH)) * D_MODEL ** -0.5
    s5_a_re = -0.5 + 0.01 * nrm(ks[3], (L, G, P))
    s5_a_im = math.pi * jnp.arange(P, dtype=jnp.float32) + 0.01 * nrm(ks[4], (L, G, P))
    s5_log_step = jax.random.uniform(ks[5], (L, G), minval=math.log(S5_STEP_MIN),
                                     maxval=math.log(S5_STEP_MAX))
    s5_b_re = nrm(ks[6], (L, G, P, I)) * (2 * I) ** -0.5
    s5_b_im = nrm(ks[7], (L, G, P, I)) * (2 * I) ** -0.5
    s5_c_re = nrm(ks[8], (L, G, I, P)) * (2 * P) ** -0.5
    s5_c_im = nrm(ks[9], (L, G, I, P)) * (2 * P) ** -0.5
    s5_d = nrm(ks[10], (L, S5_WIDTH))
    s5_glu_w = nrm(ks[11], (L, S5_WIDTH, S5_WIDTH)) * S5_WIDTH ** -0.5
    s5_glu_b = 0.01 * nrm(ks[12], (L, S5_WIDTH))
    q_norm_w = 1.0 + 0.02 * nrm(ks[13], (L, ATT_HEAD_DIM))
    k_norm_w = 1.0 + 0.02 * nrm(ks[14], (L, ATT_HEAD_DIM))
    conv_w = nrm(ks[15], (L, SSD_CONV, SSD_CONV_DIM)) * SSD_CONV ** -0.5
    conv_b = 0.01 * nrm(ks[16], (L, SSD_CONV_DIM))
    dt0 = jnp.exp(jax.random.uniform(ks[17], (L, SSD_HEADS), minval=math.log(SSD_DT_MIN),
                                     maxval=math.log(SSD_DT_MAX)))
    dt_bias = dt0 + jnp.log(-jnp.expm1(-dt0))
    ssd_a_log = jnp.log(jax.random.uniform(ks[18], (L, SSD_HEADS), minval=1.0, maxval=16.0))
    ssd_d = 1.0 + 0.1 * nrm(ks[19], (L, SSD_HEADS))
    ssd_norm_w = 1.0 + 0.02 * nrm(ks[20], (L, SSD_WIDTH))
    proj_a = nrm(ks[21], (L, S5_WIDTH, D_MODEL)) * S5_WIDTH ** -0.5
    proj_b = nrm(ks[22], (L, ATT_OUT_WIDTH, D_MODEL)) * ATT_OUT_WIDTH ** -0.5
    proj_c = nrm(ks[23], (L, SSD_WIDTH, D_MODEL)) * SSD_WIDTH ** -0.5
    w_out = nrm(ks[24], (L, D_MODEL, D_MODEL)) * (0.5 * D_MODEL ** -0.5)
    return {"x": x, "norm_w": norm_w, "w_in": w_in,
            "s5_a_re": s5_a_re, "s5_a_im": s5_a_im, "s5_log_step": s5_log_step,
            "s5_b_re": s5_b_re, "s5_b_im": s5_b_im, "s5_c_re": s5_c_re, "s5_c_im": s5_c_im,
            "s5_d": s5_d, "s5_glu_w": s5_glu_w, "s5_glu_b": s5_glu_b,
            "q_norm_w": q_norm_w, "k_norm_w": k_norm_w,
            "conv_w": conv_w, "conv_b": conv_b, "dt_bias": dt_bias,
            "ssd_a_log": ssd_a_log, "ssd_d": ssd_d, "ssd_norm_w": ssd_norm_w,
            "proj_a": proj_a, "proj_b": proj_b, "proj_c": proj_c, "w_out": w_out}


def reference(x, norm_w, w_in, s5_a_re, s5_a_im, s5_log_step, s5_b_re, s5_b_im, s5_c_re,
              s5_c_im, s5_d, s5_glu_w, s5_glu_b, q_norm_w, k_norm_w, conv_w, conv_b,
              dt_bias, ssd_a_log, ssd_d, ssd_norm_w, proj_a, proj_b, proj_c, w_out):
    for i in range(DEPTH):
        x = hybrid_layer(x, norm_w[i], w_in[i], s5_a_re[i], s5_a_im[i], s5_log_step[i],
                         s5_b_re[i], s5_b_im[i], s5_c_re[i], s5_c_im[i], s5_d[i],
                         s5_glu_w[i], s5_glu_b[i], q_norm_w[i], k_norm_w[i], conv_w[i],
                         conv_b[i], dt_bias[i], ssd_a_log[i], ssd_d[i], ssd_norm_w[i],
                         proj_a[i], proj_b[i], proj_c[i], w_out[i])
    return x
```

```python
import functools
import math

import numpy as np
import jax
import jax.numpy as jnp
from jax import lax
from jax.experimental import pallas as pl
from jax.experimental.pallas import tpu as pltpu

D_MODEL = 1024
RMS_EPS = 1e-6

S5_WIDTH = 512
S5_GROUP = 16
S5_GROUPS = 32
S5_STATE = 64
S5_CHUNK = 16
S5_PAIRS = S5_GROUPS // 2

ATT_HEAD_DIM = 64
ATT_PAIRS = ((128, 1), (512, 4), (2048, 16))
ATT_HPG = 4
ATT_WIDTH = 768
ATT_GW = ATT_HPG * ATT_HEAD_DIM
ATT_BLOCK = 128

SSD_HEAD_DIM = 64
SSD_WIDTH = 768
SSD_HEADS = 12
SSD_GROUPS = 2
SSD_STATE = 128
SSD_CONV = 4
SSD_CHUNK = 128
SSD_CONV_DIM = 1280
SSD_DT_PAD = 128
SSD_IN = SSD_CONV_DIM + SSD_DT_PAD

IN_SPLITS = (512, 512, 768, 768, 768, 256, 1280, 12, 768, 3072)
_OFF = np.concatenate([[0], np.cumsum(IN_SPLITS)]).tolist()
(O_UA, O_ZA, O_Q, O_K, O_V, O_ZB, O_XBC, O_DT, O_ZC, O_GATE, O_END) = _OFF

A_UA, A_Q, A_K, A_V, A_XBC = 0, 512, 1280, 2048, 2816
A_WIDTH = A_XBC + SSD_IN
Z_ZA, Z_ZB, Z_ZC, Z_GATE = 0, 512, 768, 1536
Z_WIDTH = Z_GATE + 3 * D_MODEL

VMEM_LIMIT = 56 * 1024 * 1024

BF16 = jnp.bfloat16
F32 = jnp.float32


def _dot(a, b):
    return jnp.dot(a, b, preferred_element_type=F32)


def _dot_nt(a, b):
    return lax.dot_general(a, b, (((1,), (1,)), ((), ())), preferred_element_type=F32)


def _const_spec(shape):
    nd = len(shape)
    return pl.BlockSpec(shape, lambda *_: (0,) * nd)


def _sigmoid(x):
    return 1.0 / (1.0 + jnp.exp(-x))


def _silu(x):
    return x * _sigmoid(x)


def _rms_rows(x, w):
    return x * lax.rsqrt(jnp.mean(x * x, axis=-1, keepdims=True) + RMS_EPS) * w


def _inproj_kernel(x_ref, nw_ref, w_ref, qw_ref, kw_ref, ones_ref,
                   ua_ref, q_ref, k_ref, v_ref, xd_ref):
    h = _rms_rows(x_ref[...], nw_ref[...]).astype(BF16)
    ua_ref[...] = _dot(h, w_ref[:, A_UA:A_Q])
    ones = ones_ref[...]
    for j in range(3):
        lo = j * ATT_GW
        qj = _dot(h, w_ref[:, A_Q + lo:A_Q + lo + ATT_GW])
        ms = _dot((qj * qj).astype(BF16), ones)
        q_ref[:, lo:lo + ATT_GW] = qj * lax.rsqrt(ms + RMS_EPS) * qw_ref[:, lo:lo + ATT_GW]
        kj = _dot(h, w_ref[:, A_K + lo:A_K + lo + ATT_GW])
        ms = _dot((kj * kj).astype(BF16), ones)
        k_ref[:, lo:lo + ATT_GW] = kj * lax.rsqrt(ms + RMS_EPS) * kw_ref[:, lo:lo + ATT_GW]
        v_ref[:, lo:lo + ATT_GW] = _dot(h, w_ref[:, A_V + lo:A_V + lo + ATT_GW])
    for j in range(SSD_IN // 128):
        lo = j * 128
        xd_ref[:, lo:lo + 128] = _dot(h, w_ref[:, A_XBC + lo:A_XBC + lo + 128])


def _inproj(x2, norm_w, w_a, qw, kw, ones, tm):
    n = x2.shape[0]
    row = lambda w: pl.BlockSpec((tm, w), lambda i: (i, 0))
    return pl.pallas_call(
        _inproj_kernel,
        grid=(n // tm,),
        in_specs=[row(D_MODEL), _const_spec((1, D_MODEL)), _const_spec((D_MODEL, A_WIDTH)),
                  _const_spec((1, ATT_WIDTH)), _const_spec((1, ATT_WIDTH)),
                  _const_spec((ATT_GW, ATT_GW))],
        out_specs=[row(S5_WIDTH), row(ATT_WIDTH), row(ATT_WIDTH), row(ATT_WIDTH), row(SSD_IN)],
        out_shape=[jax.ShapeDtypeStruct((n, w), F32)
                   for w in (S5_WIDTH, ATT_WIDTH, ATT_WIDTH, ATT_WIDTH, SSD_IN)],
        compiler_params=pltpu.CompilerParams(dimension_semantics=("parallel",),
                                             vmem_limit_bytes=VMEM_LIMIT),
        name="inproj",
    )(x2, norm_w, w_a, qw, kw, ones)


def _s5_weights(a_re, a_im, log_step, b_re, b_im, c_re, c_im, d):
    G, P, I, T = S5_GROUPS, S5_STATE, S5_GROUP, S5_CHUNK
    a_re, a_im = a_re.astype(F32), a_im.astype(F32)
    b_re, b_im = b_re.astype(F32), b_im.astype(F32)
    c_re, c_im = c_re.astype(F32), c_im.astype(F32)
    step = jnp.exp(log_step.astype(F32))[:, None]
    mag = jnp.exp(a_re * step)
    ang = a_im * step
    lam_re, lam_im = mag * jnp.cos(ang), mag * jnp.sin(ang)
    num_re, num_im = lam_re - 1.0, lam_im
    den = a_re * a_re + a_im * a_im
    f_re = (num_re * a_re + num_im * a_im) / den
    f_im = (num_im * a_re - num_re * a_im) / den
    bb_re = f_re[..., None] * b_re - f_im[..., None] * b_im
    bb_im = f_re[..., None] * b_im + f_im[..., None] * b_re
    pw_re, pw_im = [jnp.ones_like(lam_re)], [jnp.zeros_like(lam_im)]
    for _ in range(T):
        r, i = pw_re[-1], pw_im[-1]
        pw_re.append(r * lam_re - i * lam_im)
        pw_im.append(r * lam_im + i * lam_re)
    pw_re, pw_im = jnp.stack(pw_re), jnp.stack(pw_im)
    hp = lax.Precision.HIGHEST
    ps_re, ps_im = pw_re[T - 1::-1], pw_im[T - 1::-1]
    ps_re, ps_im = ps_re[:T], ps_im[:T]
    wst_re = (jnp.einsum('sgp,gpi->gsip', ps_re, bb_re, precision=hp)
              - jnp.einsum('sgp,gpi->gsip', ps_im, bb_im, precision=hp)).reshape(G, T * I, P)
    wst_im = (jnp.einsum('sgp,gpi->gsip', ps_re, bb_im, precision=hp)
              + jnp.einsum('sgp,gpi->gsip', ps_im, bb_re, precision=hp)).reshape(G, T * I, P)
    po_re, po_im = pw_re[1:T + 1], pw_im[1:T + 1]
    wo_re = (jnp.einsum('gip,tgp->gpti', c_re, po_re, precision=hp)
             - jnp.einsum('gip,tgp->gpti', c_im, po_im, precision=hp)).reshape(G, P, T * I)
    wo_im = -(jnp.einsum('gip,tgp->gpti', c_re, po_im, precision=hp)
              + jnp.einsum('gip,tgp->gpti', c_im, po_re, precision=hp)).reshape(G, P, T * I)
    cl_re = c_re[None] * pw_re[:T, :, None, :] - c_im[None] * pw_im[:T, :, None, :]
    cl_im = c_re[None] * pw_im[:T, :, None, :] + c_im[None] * pw_re[:T, :, None, :]
    kj = (jnp.einsum('jgop,gpi->jgoi', cl_re, bb_re, precision=hp)
          - jnp.einsum('jgop,gpi->jgoi', cl_im, bb_im, precision=hp))
    t_idx = np.arange(T)
    lag = t_idx[:, None] - t_idx[None, :]
    kts = kj[np.clip(lag, 0, T - 1)]
    kts = jnp.where(jnp.asarray(lag >= 0)[:, :, None, None, None], kts, 0.0)
    skip = d.astype(F32).reshape(G, I)
    eye_t = jnp.eye(T, dtype=F32)
    eye_i = jnp.eye(I, dtype=F32)
    kts = kts + (eye_t[:, :, None, None, None] * eye_i[None, None, None, :, :]
                 * skip[None, None, :, :, None])
    wtoep = kts.transpose(2, 1, 4, 0, 3).reshape(G, T * I, T * I)

    def pair_diag(w):
        a, b = w.shape[1:]
        w = w.reshape(G // 2, 2, a, b)
        z = jnp.zeros((G // 2, a, b), F32)
        top = jnp.concatenate([w[:, 0], z], axis=2)
        bot = jnp.concatenate([z, w[:, 1]], axis=2)
        return jnp.concatenate([top, bot], axis=1)

    lam = jnp.stack([pw_re[T].reshape(G // 2, 2 * P), pw_im[T].reshape(G // 2, 2 * P)], axis=1)
    return (pair_diag(wst_re).astype(BF16), pair_diag(wst_im).astype(BF16),
            pair_diag(wtoep).astype(BF16), pair_diag(wo_re).astype(BF16),
            pair_diag(wo_im).astype(BF16), lam)


def _s5_kernel(uc_ref, wsr_ref, wsi_ref, wt_ref, wor_ref, woi_ref, lam_ref, y_ref,
               hre_s, him_s, sre_s, sim_s, cre_s, cim_s, *, batch):
    rt = uc_ref.shape[1]

    @pl.when(pl.program_id(1) == 0)
    def _():
        cre_s[...] = jnp.zeros_like(cre_s)
        cim_s[...] = jnp.zeros_like(cim_s)

    u = uc_ref[0].astype(BF16)
    sre_s[...] = _dot(u, wsr_ref[0])
    sim_s[...] = _dot(u, wsi_ref[0])
    lre = jnp.broadcast_to(lam_ref[0, 0:1, :], (batch, 128))
    lim = jnp.broadcast_to(lam_ref[0, 1:2, :], (batch, 128))

    def body(n, carry):
        hr, hi = carry
        r0 = pl.multiple_of(n * batch, batch)
        hre_s[pl.ds(r0, batch), :] = hr
        him_s[pl.ds(r0, batch), :] = hi
        sr = sre_s[pl.ds(r0, batch), :]
        si = sim_s[pl.ds(r0, batch), :]
        return (lre * hr - lim * hi + sr, lre * hi + lim * hr + si)

    hr, hi = lax.fori_loop(0, rt // batch, body, (cre_s[...], cim_s[...]))
    cre_s[...] = hr
    cim_s[...] = hi
    y_ref[0] = (_dot(u, wt_ref[0]) + _dot(hre_s[...].astype(BF16), wor_ref[0])
                + _dot(him_s[...].astype(BF16), woi_ref[0]))


def _s5_scan(uc, weights, batch, rt):
    wsr, wsi, wt, wor, woi, lam = weights
    npair, rows, width = uc.shape
    wspec = lambda a, b: pl.BlockSpec((1, a, b), lambda p, j: (p, 0, 0))
    return pl.pallas_call(
        functools.partial(_s5_kernel, batch=batch),
        grid=(npair, rows // rt),
        in_specs=[pl.BlockSpec((1, rt, width), lambda p, j: (p, j, 0)),
                  wspec(512, 128), wspec(512, 128), wspec(512, 512),
                  wspec(128, 512), wspec(128, 512), wspec(2, 128)],
        out_specs=pl.BlockSpec((1, rt, width), lambda p, j: (p, j, 0)),
        out_shape=jax.ShapeDtypeStruct(uc.shape, F32),
        scratch_shapes=[pltpu.VMEM((rt, 128), F32)] * 4 + [pltpu.VMEM((batch, 128), F32)] * 2,
        compiler_params=pltpu.CompilerParams(dimension_semantics=("parallel", "arbitrary"),
                                             vmem_limit_bytes=VMEM_LIMIT),
        name="s5_scan",
    )(uc, wsr, wsi, wt, wor, woi, lam)


def _s5_branch(u_a, weights):
    b, s, _ = u_a.shape
    nc = s // S5_CHUNK
    uc = u_a.reshape(b, nc, S5_CHUNK, S5_PAIRS, 2, S5_GROUP)
    uc = uc.transpose(3, 1, 0, 4, 2, 5).reshape(S5_PAIRS, nc * b, 2 * S5_CHUNK * S5_GROUP)
    rt = min(1024, nc * b)
    y = _s5_scan(uc, weights, b, rt)
    y = y.reshape(S5_PAIRS, nc, b, 2, S5_CHUNK, S5_GROUP).transpose(2, 1, 4, 0, 3, 5)
    return y.reshape(b, s, S5_WIDTH)


def _attn_kernel(q_ref, kc_ref, vc_ref, kp_ref, vp_ref, o_ref, lse_ref, *, nqb):
    j = pl.program_id(2)
    head = lax.broadcasted_iota(jnp.int32, (1, ATT_GW), 1) // ATT_HEAD_DIM
    row = lax.broadcasted_iota(jnp.int32, (ATT_BLOCK, ATT_BLOCK), 0)
    col = lax.broadcasted_iota(jnp.int32, (ATT_BLOCK, ATT_BLOCK), 1)
    mask_cur = col <= row
    mask_prev = col >= row
    neg = jnp.float32(-jnp.inf)
    for jb in range(nqb):
        sl = slice(jb * ATT_BLOCK, (jb + 1) * ATT_BLOCK)
        q = q_ref[0, sl, :]
        kc = kc_ref[0, sl, :].astype(BF16)
        vc = vc_ref[0, sl, :]
        if jb == 0:
            kp = kp_ref[0].astype(BF16)
            vp = vp_ref[0]
            mp = jnp.logical_and(mask_prev, j > 0)
        else:
            sp_ = slice((jb - 1) * ATT_BLOCK, jb * ATT_BLOCK)
            kp = kc_ref[0, sp_, :].astype(BF16)
            vp = vc_ref[0, sp_, :]
            mp = mask_prev
        acc = jnp.zeros((ATT_BLOCK, ATT_GW), F32)
        lse = jnp.zeros((ATT_BLOCK, ATT_GW), F32)
        for h in range(ATT_HPG):
            hm = head == h
            qh = jnp.where(hm, q, 0.0).astype(BF16)
            sc = jnp.where(mask_cur, _dot_nt(qh, kc), neg)
            sp = jnp.where(mp, _dot_nt(qh, kp), neg)
            m = jnp.maximum(jnp.max(sc, axis=-1, keepdims=True),
                            jnp.max(sp, axis=-1, keepdims=True))
            pc = jnp.exp(sc - m)
            pp = jnp.exp(sp - m)
            l = jnp.sum(pc, axis=-1, keepdims=True) + jnp.sum(pp, axis=-1, keepdims=True)
            inv = 1.0 / l
            vch = jnp.where(hm, vc, 0.0).astype(BF16)
            vph = jnp.where(hm, vp, 0.0).astype(BF16)
            acc = acc + _dot((pc * inv).astype(BF16), vch) + _dot((pp * inv).astype(BF16), vph)
            lse = jnp.where(hm, m + jnp.log(l), lse)
        o_ref[0, sl, :] = acc
        lse_ref[0, sl, :] = lse


def _attn_group(q, k, v, gi, dilation):
    b, s, _ = q.shape
    ln = s // dilation
    tq = min(512, ln)
    nqb = tq // ATT_BLOCK
    view = lambda t: t.reshape(b, ln, dilation * ATT_WIDTH)
    ncol = ATT_WIDTH // ATT_GW
    cur = pl.BlockSpec((1, tq, ATT_GW), lambda bi, r, j: (bi, j, r * ncol + gi))
    prev = pl.BlockSpec((1, ATT_BLOCK, ATT_GW),
                        lambda bi, r, j: (bi, jnp.maximum(j * nqb - 1, 0), r * ncol + gi))
    out = pl.BlockSpec((1, tq, ATT_GW), lambda bi, r, j: (bi, j, r))
    o, lse = pl.pallas_call(
        functools.partial(_attn_kernel, nqb=nqb),
        grid=(b, dilation, ln // tq),
        in_specs=[cur, cur, cur, prev, prev],
        out_specs=[out, out],
        out_shape=[jax.ShapeDtypeStruct((b, ln, dilation * ATT_GW), F32)] * 2,
        compiler_params=pltpu.CompilerParams(
            dimension_semantics=("parallel", "parallel", "arbitrary"),
            vmem_limit_bytes=VMEM_LIMIT),
        name=f"attn_d{dilation}",
    )(view(q), view(k), view(v), view(k), view(v))
    return o.reshape(b, s, ATT_GW), lse.reshape(b, s, ATT_GW)


def _ssd_kernel(xd_ref, cw_ref, cb_ref, dtb_ref, alog_ref, dsk_ref, tril_ref, y_ref,
                xpad_s, state_s):
    L = SSD_CHUNK

    @pl.when(pl.program_id(1) == 0)
    def _():
        xpad_s[0:8, :] = jnp.zeros((8, SSD_CONV_DIM), F32)
        state_s[...] = jnp.zeros_like(state_s)

    xpad_s[8:8 + L, :] = xd_ref[0, :, 0:SSD_CONV_DIM]
    acc = cb_ref[...] + cw_ref[0:1, :] * xpad_s[5:5 + L, :]
    for kk in range(1, SSD_CONV):
        acc = acc + cw_ref[kk:kk + 1, :] * xpad_s[5 + kk:5 + kk + L, :]
    xpad_s[0:8, :] = xpad_s[L:L + 8, :]
    xc = _silu(acc)

    lane = lax.broadcasted_iota(jnp.int32, (1, 128), 1)
    dtr = xd_ref[0, :, SSD_CONV_DIM:SSD_IN] + dtb_ref[...]
    dt = jnp.maximum(dtr, 0.0) + jnp.log(1.0 + jnp.exp(-jnp.abs(dtr)))
    a = jnp.where(lane < SSD_HEADS, -jnp.exp(alog_ref[...]), 0.0)
    a_dt = dt * a
    a_cs = jnp.dot(tril_ref[...], a_dt, preferred_element_type=F32,
                   precision=lax.Precision.HIGHEST)
    a_cs_t = a_cs.T
    dt_t = dt.T
    last_t = a_cs_t[:, L - 1:L]

    row = lax.broadcasted_iota(jnp.int32, (L, L), 0)
    col = lax.broadcasted_iota(jnp.int32, (L, L), 1)
    tri = row >= col
    lane_lo = lane < SSD_HEAD_DIM
    neg = jnp.float32(-jnp.inf)

    for g in range(SSD_GROUPS):
        bm = xc[:, SSD_WIDTH + g * SSD_STATE:SSD_WIDTH + (g + 1) * SSD_STATE]
        cm = xc[:, SSD_WIDTH + (SSD_GROUPS + g) * SSD_STATE:
                SSD_WIDTH + (SSD_GROUPS + g + 1) * SSD_STATE].astype(BF16)
        cb = _dot_nt(cm, bm.astype(BF16))
        bm_t = bm.T
        for pr in range(3 * g, 3 * g + 3):
            xs = xc[:, pr * 128:(pr + 1) * 128]
            x_lo = jnp.where(lane_lo, xs, 0.0).astype(BF16)
            x_hi = jnp.where(lane_lo, 0.0, xs).astype(BF16)
            mats, bws, cols, decs = [], [], [], []
            for h in (2 * pr, 2 * pr + 1):
                c_col = a_cs[:, h:h + 1]
                c_row = a_cs_t[h:h + 1, :]
                d_row = dt_t[h:h + 1, :]
                dec = jnp.exp(jnp.where(tri, c_col - c_row, neg))
                mats.append((cb * dec * d_row).astype(BF16))
                w_row = jnp.exp(last_t[h:h + 1, :] - c_row) * d_row
                bws.append((bm_t * w_row).astype(BF16))
                cols.append(jnp.exp(c_col))
                decs.append(jnp.exp(last_t[h:h + 1, :]))
            st = state_s[pr]
            y_diag = _dot(mats[0], x_lo) + _dot(mats[1], x_hi)
            y_off = _dot(cm, st.astype(BF16)) * jnp.where(lane_lo, cols[0], cols[1])
            y_ref[0, :, pr * 128:(pr + 1) * 128] = (
                y_diag + y_off + xs * dsk_ref[:, pr * 128:(pr + 1) * 128])
            inc = _dot(bws[0], x_lo) + _dot(bws[1], x_hi)
            state_s[pr] = st * jnp.where(lane_lo, decs[0], decs[1]) + inc


def _ssd_scan(xd, conv_w, conv_b, dt_bias, a_log, d):
    b, s, _ = xd.shape
    pad = lambda t: jnp.pad(t.astype(F32), (0, SSD_DT_PAD - SSD_HEADS)).reshape(1, SSD_DT_PAD)
    dsk = jnp.repeat(d.astype(F32), SSD_HEAD_DIM).reshape(1, SSD_WIDTH)
    tril = jnp.asarray(np.tril(np.ones((SSD_CHUNK, SSD_CHUNK), np.float32)))
    return pl.pallas_call(
        _ssd_kernel,
        grid=(b, s // SSD_CHUNK),
        in_specs=[pl.BlockSpec((1, SSD_CHUNK, SSD_IN), lambda bi, c: (bi, c, 0)),
                  _const_spec((SSD_CONV, SSD_CONV_DIM)), _const_spec((1, SSD_CONV_DIM)),
                  _const_spec((1, SSD_DT_PAD)), _const_spec((1, SSD_DT_PAD)),
                  _const_spec((1, SSD_WIDTH)), _const_spec((SSD_CHUNK, SSD_CHUNK))],
        out_specs=pl.BlockSpec((1, SSD_CHUNK, SSD_WIDTH), lambda bi, c: (bi, c, 0)),
        out_shape=jax.ShapeDtypeStruct((b, s, SSD_WIDTH), F32),
        scratch_shapes=[pltpu.VMEM((SSD_CHUNK + 8, SSD_CONV_DIM), F32),
                        pltpu.VMEM((SSD_HEADS // 2, SSD_STATE, 128), F32)],
        compiler_params=pltpu.CompilerParams(dimension_semantics=("parallel", "arbitrary"),
                                             vmem_limit_bytes=VMEM_LIMIT),
        name="ssd_scan",
    )(xd, conv_w.astype(F32), conv_b.astype(F32).reshape(1, SSD_CONV_DIM),
      pad(dt_bias), pad(a_log), dsk, tril)


def _gelu_tanh(x):
    c = math.sqrt(2.0 / math.pi)
    return 0.5 * x * (1.0 + jnp.tanh(c * (x + 0.044715 * (x * x * x))))


def _merge_kernel(x_ref, nw_ref, wz_ref, ya_ref, o0_ref, o1_ref, o2_ref, l0_ref, l1_ref, l2_ref,
                  yc_ref, gw_ref, gb_ref, snw_ref, pa_ref, pb_ref, pc_ref, wo_ref, out_ref):
    x = x_ref[...]
    h = _rms_rows(x, nw_ref[...]).astype(BF16)

    def gate(i):
        lo = Z_GATE + i * D_MODEL
        return _sigmoid(_dot(h, wz_ref[:, lo:lo + D_MODEL]))

    g = _gelu_tanh(ya_ref[...])
    y_a = g * _sigmoid(_dot(g.astype(BF16), gw_ref[...]) + gb_ref[...])
    y_a = y_a * _silu(_dot(h, wz_ref[:, Z_ZA:Z_ZB]))
    merged = gate(0) * _dot(y_a.astype(BF16), pa_ref[...])

    l0, l1, l2 = l0_ref[...], l1_ref[...], l2_ref[...]
    m = jnp.maximum(jnp.maximum(l0, l1), l2)
    e0, e1, e2 = jnp.exp(l0 - m), jnp.exp(l1 - m), jnp.exp(l2 - m)
    y_b = (e0 * o0_ref[...] + e1 * o1_ref[...] + e2 * o2_ref[...]) / (e0 + e1 + e2)
    y_b = y_b * _silu(_dot(h, wz_ref[:, Z_ZB:Z_ZC]))
    merged = merged + gate(1) * _dot(y_b.astype(BF16), pb_ref[...])

    y_c = yc_ref[...] * _silu(_dot(h, wz_ref[:, Z_ZC:Z_GATE]))
    y_c = _rms_rows(y_c, snw_ref[...])
    merged = merged + gate(2) * _dot(y_c.astype(BF16), pc_ref[...])

    out_ref[...] = x + _dot(merged.astype(BF16), wo_ref[...])


def _merge(x2, norm_w, w_z, ya, os_, ls_, yc, glu_w, glu_b, ssd_nw, pa, pb, pc, wo, tm):
    n = x2.shape[0]
    row = lambda w: pl.BlockSpec((tm, w), lambda i: (i, 0))
    return pl.pallas_call(
        _merge_kernel,
        grid=(n // tm,),
        in_specs=[row(D_MODEL), _const_spec((1, D_MODEL)), _const_spec((D_MODEL, Z_WIDTH)),
                  row(S5_WIDTH)] + [row(ATT_GW)] * 6 + [row(SSD_WIDTH),
                  _const_spec((S5_WIDTH, S5_WIDTH)), _const_spec((1, S5_WIDTH)),
                  _const_spec((1, SSD_WIDTH)), _const_spec((S5_WIDTH, D_MODEL)),
                  _const_spec((ATT_GW, D_MODEL)), _const_spec((SSD_WIDTH, D_MODEL)),
                  _const_spec((D_MODEL, D_MODEL))],
        out_specs=row(D_MODEL),
        out_shape=jax.ShapeDtypeStruct((n, D_MODEL), F32),
        compiler_params=pltpu.CompilerParams(dimension_semantics=("parallel",),
                                             vmem_limit_bytes=VMEM_LIMIT),
        name="merge",
    )(x2, norm_w, w_z, ya, *os_, *ls_, yc, glu_w, glu_b, ssd_nw, pa, pb, pc, wo)


def _layer(x, norm_w, w_in, s5_a_re, s5_a_im, s5_log_step, s5_b_re, s5_b_im, s5_c_re,
           s5_c_im, s5_d, s5_glu_w, s5_glu_b, q_norm_w, k_norm_w, conv_w, conv_b,
           dt_bias, ssd_a_log, ssd_d, ssd_norm_w, proj_a, proj_b, proj_c, w_out):
    b, s, _ = x.shape
    n = b * s
    x2 = x.reshape(n, D_MODEL)
    nw = norm_w.astype(F32).reshape(1, D_MODEL)
    w_in = w_in.astype(F32)
    w_a = jnp.concatenate(
        [w_in[:, O_UA:O_ZA], w_in[:, O_Q:O_ZB], w_in[:, O_XBC:O_ZC],
         jnp.zeros((D_MODEL, SSD_DT_PAD - SSD_HEADS), F32)], axis=1).astype(BF16)
    w_z = jnp.concatenate(
        [w_in[:, O_ZA:O_Q], w_in[:, O_ZB:O_XBC], w_in[:, O_ZC:O_END]], axis=1).astype(BF16)
    qw = jnp.tile(q_norm_w.astype(F32) * (ATT_HEAD_DIM ** -0.5), ATT_WIDTH // ATT_HEAD_DIM)
    kw = jnp.tile(k_norm_w.astype(F32), ATT_WIDTH // ATT_HEAD_DIM)
    blk = np.arange(ATT_GW) // ATT_HEAD_DIM
    ones = jnp.asarray((blk[:, None] == blk[None, :]).astype(np.float32) / ATT_HEAD_DIM, BF16)

    tm = 256
    u_a, q, k, v, xd = _inproj(x2, nw, w_a, qw.reshape(1, -1), kw.reshape(1, -1), ones, tm)

    s5w = _s5_weights(s5_a_re, s5_a_im, s5_log_step, s5_b_re, s5_b_im, s5_c_re, s5_c_im, s5_d)
    y_a = _s5_branch(u_a.reshape(b, s, S5_WIDTH), s5w).reshape(n, S5_WIDTH)

    q3, k3, v3 = (t.reshape(b, s, ATT_WIDTH) for t in (q, k, v))
    os_, ls_ = [], []
    for gi, (_, dilation) in enumerate(ATT_PAIRS):
        o, lse = _attn_group(q3, k3, v3, gi, dilation)
        os_.append(o.reshape(n, ATT_GW))
        ls_.append(lse.reshape(n, ATT_GW))

    y_c = _ssd_scan(xd.reshape(b, s, SSD_IN), conv_w, conv_b, dt_bias, ssd_a_log, ssd_d)

    out = _merge(x2, nw, w_z, y_a, os_, ls_, y_c.reshape(n, SSD_WIDTH),
                 s5_glu_w.astype(BF16), s5_glu_b.astype(F32).reshape(1, S5_WIDTH),
                 ssd_norm_w.astype(F32).reshape(1, SSD_WIDTH),
                 proj_a.astype(BF16), proj_b.astype(BF16), proj_c.astype(BF16),
                 w_out.astype(BF16), tm)
    return out.reshape(b, s, D_MODEL)


def kernel(x, norm_w, w_in, s5_a_re, s5_a_im, s5_log_step, s5_b_re, s5_b_im, s5_c_re, s5_c_im,
           s5_d, s5_glu_w, s5_glu_b, q_norm_w, k_norm_w, conv_w, conv_b, dt_bias, ssd_a_log,
           ssd_d, ssd_norm_w, proj_a, proj_b, proj_c, w_out):
    params = (norm_w, w_in, s5_a_re, s5_a_im, s5_log_step, s5_b_re, s5_b_im, s5_c_re, s5_c_im,
              s5_d, s5_glu_w, s5_glu_b, q_norm_w, k_norm_w, conv_w, conv_b, dt_bias, ssd_a_log,
              ssd_d, ssd_norm_w, proj_a, proj_b, proj_c, w_out)
    for i in range(norm_w.shape[0]):
        x = _layer(x, *(p[i] for p in params))
    return x
```

```python
import functools
import math

import numpy as np
import jax
import jax.numpy as jnp
from jax import lax
from jax.experimental import pallas as pl
from jax.experimental.pallas import tpu as pltpu

D_MODEL = 1024
RMS_EPS = 1e-6
LANES = 128

S5_WIDTH = 512
S5_GROUP = 16
S5_GROUPS = 32
S5_STATE = 64
S5_CHUNK = 16
S5_PAIRS = S5_GROUPS // 2
S5_TILES = S5_WIDTH // LANES
S5_PPT = S5_PAIRS // S5_TILES
S5_PW = 2 * S5_GROUP
S5_TT = 512

ATT_HEAD_DIM = 64
ATT_PAIRS = ((128, 1), (512, 4), (2048, 16))
ATT_HPG = 4
ATT_WIDTH = 768
ATT_GW = ATT_HPG * ATT_HEAD_DIM
ATT_BLOCK = 128
ATT_TILES = ATT_WIDTH // LANES
ATT_OUT_TILES = ATT_GW // LANES
ATT_TT = 2048

SSD_HEAD_DIM = 64
SSD_WIDTH = 768
SSD_HEADS = 12
SSD_GROUPS = 2
SSD_STATE = 128
SSD_CONV = 4
SSD_CHUNK = 128
SSD_CONV_DIM = 1280
SSD_DT_PAD = 128
SSD_IN = SSD_CONV_DIM + SSD_DT_PAD

IN_SPLITS = (512, 512, 768, 768, 768, 256, 1280, 12, 768, 3072)
_OFF = np.concatenate([[0], np.cumsum(IN_SPLITS)]).tolist()
(O_UA, O_ZA, O_Q, O_K, O_V, O_ZB, O_XBC, O_DT, O_ZC, O_GATE, O_END) = _OFF

A_UA, A_Q, A_K, A_V, A_XBC = 0, 512, 1280, 2048, 2816
A_WIDTH = A_XBC + SSD_IN
Z_ZA, Z_ZB, Z_ZC, Z_GATE = 0, 512, 768, 1536
Z_WIDTH = Z_GATE + 3 * D_MODEL

ROW_TILE = 256
VMEM_LIMIT = 56 * 1024 * 1024

BF16 = jnp.bfloat16
F32 = jnp.float32


def _dot(a, b):
    return jnp.dot(a, b, preferred_element_type=F32)


def _dot_nt(a, b):
    return lax.dot_general(a, b, (((1,), (1,)), ((), ())), preferred_element_type=F32)


def _const_spec(shape):
    nd = len(shape)
    return pl.BlockSpec(shape, lambda *_: (0,) * nd, pipeline_mode=pl.Buffered(1))


def _sigmoid(x):
    return 1.0 / (1.0 + jnp.exp(-x))


def _silu(x):
    return x * _sigmoid(x)


def _rms_rows(x, w):
    return x * lax.rsqrt(jnp.mean(x * x, axis=-1, keepdims=True) + RMS_EPS) * w


def _inproj_kernel(x_ref, nw_ref, w_ref, qw_ref, kw_ref, ones_ref,
                   ua_ref, q_ref, k_ref, v_ref, xd_ref):
    h = _rms_rows(x_ref[...], nw_ref[...]).astype(BF16)
    for c in range(S5_TILES):
        ua_ref[c] = _dot(h, w_ref[:, A_UA + c * LANES:A_UA + (c + 1) * LANES])
    ones = ones_ref[...]

    def normed(col, gain):
        t = _dot(h, w_ref[:, col:col + ATT_GW])
        ms = _dot((t * t).astype(BF16), ones)
        return t * lax.rsqrt(ms + RMS_EPS) * gain

    for j in range(ATT_WIDTH // ATT_GW):
        lo = j * ATT_GW
        qj = normed(A_Q + lo, qw_ref[:, lo:lo + ATT_GW])
        kj = normed(A_K + lo, kw_ref[:, lo:lo + ATT_GW])
        vj = _dot(h, w_ref[:, A_V + lo:A_V + lo + ATT_GW])
        for half in range(ATT_GW // LANES):
            sl = slice(half * LANES, (half + 1) * LANES)
            q_ref[2 * j + half] = qj[:, sl]
            k_ref[2 * j + half] = kj[:, sl]
            v_ref[2 * j + half] = vj[:, sl]
    for j in range(SSD_IN // LANES):
        lo = j * LANES
        xd_ref[:, lo:lo + LANES] = _dot(h, w_ref[:, A_XBC + lo:A_XBC + lo + LANES])


def _time_major_block(i, batch, seq, tm, tt):
    per_seq = seq // tm
    b, r = i // per_seq, i % per_seq
    sub = tt // tm
    return ((r // sub) * batch + b) * sub + r % sub


def _inproj(x2, norm_w, w_a, qw, kw, ones, batch, seq):
    n = x2.shape[0]
    tm = ROW_TILE
    row = lambda w: pl.BlockSpec((tm, w), lambda i: (i, 0))
    tiled = lambda t: pl.BlockSpec((t, tm, LANES), lambda i: (0, i, 0))
    s5_tiled = pl.BlockSpec(
        (S5_TILES, tm, LANES),
        lambda i: (0, _time_major_block(i, batch, seq, tm, min(S5_TT, seq)), 0))
    return pl.pallas_call(
        _inproj_kernel,
        grid=(n // tm,),
        in_specs=[row(D_MODEL), _const_spec((1, D_MODEL)), _const_spec((D_MODEL, A_WIDTH)),
                  _const_spec((1, ATT_WIDTH)), _const_spec((1, ATT_WIDTH)),
                  _const_spec((ATT_GW, ATT_GW))],
        out_specs=[s5_tiled, tiled(ATT_TILES), tiled(ATT_TILES), tiled(ATT_TILES), row(SSD_IN)],
        out_shape=[jax.ShapeDtypeStruct((S5_TILES, n, LANES), F32),
                   jax.ShapeDtypeStruct((ATT_TILES, n, LANES), F32),
                   jax.ShapeDtypeStruct((ATT_TILES, n, LANES), F32),
                   jax.ShapeDtypeStruct((ATT_TILES, n, LANES), F32),
                   jax.ShapeDtypeStruct((n, SSD_IN), F32)],
        compiler_params=pltpu.CompilerParams(dimension_semantics=("parallel",),
                                             vmem_limit_bytes=VMEM_LIMIT),
        name="inproj",
    )(x2, norm_w, w_a, qw, kw, ones)


def _s5_weights(a_re, a_im, log_step, b_re, b_im, c_re, c_im, d):
    G, P, I, T = S5_GROUPS, S5_STATE, S5_GROUP, S5_CHUNK
    a_re, a_im = a_re.astype(F32), a_im.astype(F32)
    b_re, b_im = b_re.astype(F32), b_im.astype(F32)
    c_re, c_im = c_re.astype(F32), c_im.astype(F32)
    step = jnp.exp(log_step.astype(F32))[:, None]
    mag = jnp.exp(a_re * step)
    ang = a_im * step
    lam_re, lam_im = mag * jnp.cos(ang), mag * jnp.sin(ang)
    num_re, num_im = lam_re - 1.0, lam_im
    den = a_re * a_re + a_im * a_im
    f_re = (num_re * a_re + num_im * a_im) / den
    f_im = (num_im * a_re - num_re * a_im) / den
    bb_re = f_re[..., None] * b_re - f_im[..., None] * b_im
    bb_im = f_re[..., None] * b_im + f_im[..., None] * b_re
    pw_re, pw_im = [jnp.ones_like(lam_re)], [jnp.zeros_like(lam_im)]
    for _ in range(T):
        r, i = pw_re[-1], pw_im[-1]
        pw_re.append(r * lam_re - i * lam_im)
        pw_im.append(r * lam_im + i * lam_re)
    pw_re, pw_im = jnp.stack(pw_re), jnp.stack(pw_im)
    hp = lax.Precision.HIGHEST
    ps_re, ps_im = pw_re[T - 1::-1], pw_im[T - 1::-1]
    wst_re = (jnp.einsum('sgp,gpi->gsip', ps_re, bb_re, precision=hp)
              - jnp.einsum('sgp,gpi->gsip', ps_im, bb_im, precision=hp))
    wst_im = (jnp.einsum('sgp,gpi->gsip', ps_re, bb_im, precision=hp)
              + jnp.einsum('sgp,gpi->gsip', ps_im, bb_re, precision=hp))
    po_re, po_im = pw_re[1:T + 1], pw_im[1:T + 1]
    wo_re = (jnp.einsum('gip,tgp->gpti', c_re, po_re, precision=hp)
             - jnp.einsum('gip,tgp->gpti', c_im, po_im, precision=hp))
    wo_im = -(jnp.einsum('gip,tgp->gpti', c_re, po_im, precision=hp)
              + jnp.einsum('gip,tgp->gpti', c_im, po_re, precision=hp))
    cl_re = c_re[None] * pw_re[:T, :, None, :] - c_im[None] * pw_im[:T, :, None, :]
    cl_im = c_re[None] * pw_im[:T, :, None, :] + c_im[None] * pw_re[:T, :, None, :]
    kj = (jnp.einsum('jgop,gpi->jgoi', cl_re, bb_re, precision=hp)
          - jnp.einsum('jgop,gpi->jgoi', cl_im, bb_im, precision=hp))
    t_idx = np.arange(T)
    lag = t_idx[:, None] - t_idx[None, :]
    kts = kj[np.clip(lag, 0, T - 1)]
    kts = jnp.where(jnp.asarray(lag >= 0)[:, :, None, None, None], kts, 0.0)
    skip = d.astype(F32).reshape(G, I)
    eye_t = jnp.eye(T, dtype=F32)
    eye_i = jnp.eye(I, dtype=F32)
    kts = kts + (eye_t[:, :, None, None, None] * eye_i[None, None, None, :, :]
                 * skip[None, None, :, :, None])
    wtoep = kts.transpose(2, 1, 4, 0, 3)

    eye2 = jnp.eye(2, dtype=F32)
    pairs = lambda w: w.reshape((S5_PAIRS, 2) + w.shape[1:])
    rows = T * 2 * I
    wst_re = jnp.einsum('rgsip,gh->rsgihp', pairs(wst_re), eye2).reshape(S5_PAIRS, rows, 2 * P)
    wst_im = jnp.einsum('rgsip,gh->rsgihp', pairs(wst_im), eye2).reshape(S5_PAIRS, rows, 2 * P)
    wtoep = jnp.einsum('rgsito,gh->rsgitho', pairs(wtoep), eye2).reshape(S5_PAIRS, rows, rows)
    wo_re = jnp.einsum('rgpto,gh->rgptho', pairs(wo_re), eye2).reshape(S5_PAIRS, 2 * P, rows)
    wo_im = jnp.einsum('rgpto,gh->rgptho', pairs(wo_im), eye2).reshape(S5_PAIRS, 2 * P, rows)
    lam = jnp.stack([pw_re[T].reshape(S5_PAIRS, 2 * P), pw_im[T].reshape(S5_PAIRS, 2 * P)],
                    axis=1)
    return (wst_re.astype(BF16), wst_im.astype(BF16), wtoep.astype(BF16),
            wo_re.astype(BF16), wo_im.astype(BF16), lam)


def _s5_kernel(u_ref, wsr_ref, wsi_ref, wt_ref, wor_ref, woi_ref, lam_ref, y_ref,
               uc_s, hre_s, him_s, sre_s, sim_s, cre_s, cim_s, *, batch, tt):
    nch = tt // S5_CHUNK
    quads = S5_CHUNK * S5_PW // LANES
    per_q = LANES // S5_PW

    @pl.when(pl.program_id(1) == 0)
    def _():
        cre_s[...] = jnp.zeros_like(cre_s)
        cim_s[...] = jnp.zeros_like(cim_s)

    u2 = u_ref.at[0]
    y2 = y_ref.at[0]
    lane = lax.broadcasted_iota(jnp.int32, (1, LANES), 1)
    window = [jnp.logical_and(lane >= k * S5_PW, lane < (k + 1) * S5_PW) for k in range(per_q)]

    def gather(n, carry):
        r0 = pl.multiple_of(n * batch, batch)
        for q in range(quads):
            outs = [None] * S5_PPT
            for k in range(per_q):
                v = u2[pl.ds(n * S5_CHUNK + q * per_q + k, batch, stride=tt), :]
                for pp in range(S5_PPT):
                    sh = ((k - pp) * S5_PW) % LANES
                    piece = v if sh == 0 else pltpu.roll(v, sh, axis=1)
                    outs[pp] = piece if k == 0 else jnp.where(window[k], piece, outs[pp])
            for pp in range(S5_PPT):
                uc_s[pp, pl.ds(r0, batch), q * LANES:(q + 1) * LANES] = outs[pp]
        return carry

    lax.fori_loop(0, nch, gather, 0)

    for pp in range(S5_PPT):
        u = uc_s[pp].astype(BF16)
        sre_s[...] = _dot(u, wsr_ref[pp])
        sim_s[...] = _dot(u, wsi_ref[pp])
        lre = jnp.broadcast_to(lam_ref[pp, 0:1, :], (batch, LANES))
        lim = jnp.broadcast_to(lam_ref[pp, 1:2, :], (batch, LANES))

        def scan(n, carry):
            hr, hi = carry
            r0 = pl.multiple_of(n * batch, batch)
            hre_s[pl.ds(r0, batch), :] = hr
            him_s[pl.ds(r0, batch), :] = hi
            sr = sre_s[pl.ds(r0, batch), :]
            si = sim_s[pl.ds(r0, batch), :]
            return (lre * hr - lim * hi + sr, lre * hi + lim * hr + si)

        hr, hi = lax.fori_loop(0, nch, scan, (cre_s[pp], cim_s[pp]))
        cre_s[pp] = hr
        cim_s[pp] = hi
        uc_s[pp] = (_dot(u, wt_ref[pp]) + _dot(hre_s[...].astype(BF16), wor_ref[pp])
                    + _dot(him_s[...].astype(BF16), woi_ref[pp]))

    def scatter(n, carry):
        r0 = pl.multiple_of(n * batch, batch)
        for q in range(quads):
            ys = [uc_s[pp, pl.ds(r0, batch), q * LANES:(q + 1) * LANES] for pp in range(S5_PPT)]
            for k in range(per_q):
                out = None
                for pp in range(S5_PPT):
                    sh = ((pp - k) * S5_PW) % LANES
                    piece = ys[pp] if sh == 0 else pltpu.roll(ys[pp], sh, axis=1)
                    out = piece if pp == 0 else jnp.where(window[pp], piece, out)
                y2[pl.ds(n * S5_CHUNK + q * per_q + k, batch, stride=tt), :] = out
        return carry

    lax.fori_loop(0, nch, scatter, 0)


def _s5_scan(u4, weights, b, s):
    wsr, wsi, wt, wor, woi, lam = weights
    tt = min(S5_TT, s)
    rows = tt // S5_CHUNK * b
    width = S5_CHUNK * S5_PW
    act = pl.BlockSpec((1, b * tt, LANES), lambda c, j: (c, j, 0))
    wspec = lambda a_, b_: pl.BlockSpec((S5_PPT, a_, b_), lambda c, j: (c, 0, 0))
    return pl.pallas_call(
        functools.partial(_s5_kernel, batch=b, tt=tt),
        grid=(S5_TILES, s // tt),
        in_specs=[act, wspec(width, LANES), wspec(width, LANES), wspec(width, width),
                  wspec(LANES, width), wspec(LANES, width), wspec(2, LANES)],
        out_specs=act,
        out_shape=jax.ShapeDtypeStruct(u4.shape, F32),
        scratch_shapes=[pltpu.VMEM((S5_PPT, rows, width), F32)]
        + [pltpu.VMEM((rows, LANES), F32)] * 4 + [pltpu.VMEM((S5_PPT, b, LANES), F32)] * 2,
        compiler_params=pltpu.CompilerParams(dimension_semantics=("parallel", "arbitrary"),
                                             vmem_limit_bytes=VMEM_LIMIT),
        name="s5_scan",
    )(u4, wsr, wsi, wt, wor, woi, lam)


def _attn_kernel(q_ref, kc_ref, vc_ref, kp_ref, vp_ref, o_ref, m_s, l_s, n_s, *, tt):
    tile = pl.program_id(1)
    group = pl.program_id(2) % len(ATT_PAIRS)
    qv, kcv, vcv = q_ref.at[0, 0], kc_ref.at[0, 0], vc_ref.at[0, 0]
    kpv, vpv, ov = kp_ref.at[0, 0], vp_ref.at[0, 0], o_ref.at[0, 0]

    lane_lo = lax.broadcasted_iota(jnp.int32, (1, LANES), 1) < ATT_HEAD_DIM
    row = lax.broadcasted_iota(jnp.int32, (2 * ATT_BLOCK, ATT_BLOCK), 0) % ATT_BLOCK
    col = lax.broadcasted_iota(jnp.int32, (2 * ATT_BLOCK, ATT_BLOCK), 1)
    mask_cur = col <= row
    mask_prev = col >= row
    neg = jnp.float32(-jnp.inf)

    def halves(t):
        return jnp.where(lane_lo, t[:ATT_BLOCK], t[ATT_BLOCK:])

    def block(i, dil, gi, from_prev_tile):
        seg = ATT_BLOCK * dil
        base = (i // dil) * seg + (i % dil)
        rows = pl.ds(base, ATT_BLOCK, stride=dil)
        q = qv[rows, :]
        kc = kcv[rows, :].astype(BF16)
        vc = vcv[rows, :].astype(BF16)
        if from_prev_tile:
            prow = pl.ds(base + tt - seg, ATT_BLOCK, stride=dil)
            kp = kpv[prow, :].astype(BF16)
            vp = vpv[prow, :].astype(BF16)
            mp = jnp.logical_and(mask_prev, tile > 0)
        else:
            prow = pl.ds(base - seg, ATT_BLOCK, stride=dil)
            kp = kcv[prow, :].astype(BF16)
            vp = vcv[prow, :].astype(BF16)
            mp = mask_prev
        q2 = jnp.concatenate([jnp.where(lane_lo, q, 0.0), jnp.where(lane_lo, 0.0, q)],
                             axis=0).astype(BF16)
        sc = jnp.where(mask_cur, _dot_nt(q2, kc), neg)
        sp = jnp.where(mp, _dot_nt(q2, kp), neg)
        m = jnp.maximum(jnp.max(sc, axis=-1, keepdims=True),
                        jnp.max(sp, axis=-1, keepdims=True))
        pc = jnp.exp(sc - m)
        pp = jnp.exp(sp - m)
        l = jnp.sum(pc, axis=-1, keepdims=True) + jnp.sum(pp, axis=-1, keepdims=True)
        acc = halves(_dot(pc.astype(BF16), vc) + _dot(pp.astype(BF16), vp))
        mb = halves(jnp.broadcast_to(m, (2 * ATT_BLOCK, LANES)))
        lb = halves(jnp.broadcast_to(l, (2 * ATT_BLOCK, LANES)))
        if gi > 0:
            m_old, l_old, n_old = m_s[rows, :], l_s[rows, :], n_s[rows, :]
            m_new = jnp.maximum(m_old, mb)
            a_old = jnp.exp(m_old - m_new)
            a_new = jnp.exp(mb - m_new)
            acc = n_old * a_old + acc * a_new
            lb = l_old * a_old + lb * a_new
            mb = m_new
        if gi == len(ATT_PAIRS) - 1:
            ov[rows, :] = acc / lb
        else:
            m_s[rows, :] = mb
            l_s[rows, :] = lb
            n_s[rows, :] = acc

    nblk = tt // ATT_BLOCK
    for gi, (_, dil) in enumerate(ATT_PAIRS):
        @pl.when(group == gi)
        def _(gi=gi, dil=dil):
            first = min(dil, nblk)

            def head(i, c):
                block(i, dil, gi, True)
                return c

            def rest(i, c):
                block(i, dil, gi, False)
                return c

            lax.fori_loop(0, first, head, 0)
            if first < nblk:
                lax.fori_loop(first, nblk, rest, 0)


def _attention(q6, k6, v6):
    _, b, s, _ = q6.shape
    tt = min(ATT_TT, s)
    ng = len(ATT_PAIRS)
    src = lambda i: (i % ng) * ATT_OUT_TILES + i // ng
    blk = (1, 1, tt, LANES)
    cur = pl.BlockSpec(blk, lambda bi, t, i: (src(i), bi, t, 0))
    prev = pl.BlockSpec(blk, lambda bi, t, i: (src(i), bi, jnp.maximum(t - 1, 0), 0))
    out = pl.BlockSpec(blk, lambda bi, t, i: (i // ng, bi, t, 0))
    return pl.pallas_call(
        functools.partial(_attn_kernel, tt=tt),
        grid=(b, s // tt, ATT_TILES),
        in_specs=[cur, cur, cur, prev, prev],
        out_specs=out,
        out_shape=jax.ShapeDtypeStruct((ATT_OUT_TILES, b, s, LANES), F32),
        scratch_shapes=[pltpu.VMEM((tt, LANES), F32)] * 3,
        compiler_params=pltpu.CompilerParams(
            dimension_semantics=("parallel", "arbitrary", "arbitrary"),
            vmem_limit_bytes=VMEM_LIMIT),
        name="attention",
    )(q6, k6, v6, k6, v6)


def _ssd_kernel(xd_ref, cw_ref, cb_ref, dtb_ref, alog_ref, dsk_ref, tril_ref, y_ref,
                xpad_s, state_s):
    L = SSD_CHUNK

    @pl.when(pl.program_id(1) == 0)
    def _():
        xpad_s[0:8, :] = jnp.zeros((8, SSD_CONV_DIM), F32)
        state_s[...] = jnp.zeros_like(state_s)

    xpad_s[8:8 + L, :] = xd_ref[0, :, 0:SSD_CONV_DIM]
    acc = cb_ref[...] + cw_ref[0:1, :] * xpad_s[5:5 + L, :]
    for kk in range(1, SSD_CONV):
        acc = acc + cw_ref[kk:kk + 1, :] * xpad_s[5 + kk:5 + kk + L, :]
    xpad_s[0:8, :] = xpad_s[L:L + 8, :]
    xc = _silu(acc)

    lane = lax.broadcasted_iota(jnp.int32, (1, LANES), 1)
    dtr = xd_ref[0, :, SSD_CONV_DIM:SSD_IN] + dtb_ref[...]
    dt = jnp.maximum(dtr, 0.0) + jnp.log(1.0 + jnp.exp(-jnp.abs(dtr)))
    a = jnp.where(lane < SSD_HEADS, -jnp.exp(alog_ref[...]), 0.0)
    a_dt = dt * a
    a_cs = jnp.dot(tril_ref[...], a_dt, preferred_element_type=F32,
                   precision=lax.Precision.HIGHEST)
    a_cs_t = a_cs.T
    dt_t = dt.T
    last_t = a_cs_t[:, L - 1:L]

    row = lax.broadcasted_iota(jnp.int32, (L, L), 0)
    col = lax.broadcasted_iota(jnp.int32, (L, L), 1)
    tri = row >= col
    lane_lo = lane < SSD_HEAD_DIM
    neg = jnp.float32(-jnp.inf)

    for g in range(SSD_GROUPS):
        bm = xc[:, SSD_WIDTH + g * SSD_STATE:SSD_WIDTH + (g + 1) * SSD_STATE]
        cm = xc[:, SSD_WIDTH + (SSD_GROUPS + g) * SSD_STATE:
                SSD_WIDTH + (SSD_GROUPS + g + 1) * SSD_STATE].astype(BF16)
        cb = _dot_nt(cm, bm.astype(BF16))
        bm_t = bm.T
        for pr in range(3 * g, 3 * g + 3):
            xs = xc[:, pr * LANES:(pr + 1) * LANES]
            x_lo = jnp.where(lane_lo, xs, 0.0).astype(BF16)
            x_hi = jnp.where(lane_lo, 0.0, xs).astype(BF16)
            mats, bws, cols, decs = [], [], [], []
            for h in (2 * pr, 2 * pr + 1):
                c_col = a_cs[:, h:h + 1]
                c_row = a_cs_t[h:h + 1, :]
                d_row = dt_t[h:h + 1, :]
                dec = jnp.exp(jnp.where(tri, c_col - c_row, neg))
                mats.append((cb * dec * d_row).astype(BF16))
                w_row = jnp.exp(last_t[h:h + 1, :] - c_row) * d_row
                bws.append((bm_t * w_row).astype(BF16))
                cols.append(jnp.exp(c_col))
                decs.append(jnp.exp(last_t[h:h + 1, :]))
            st = state_s[pr]
            y_diag = _dot(mats[0], x_lo) + _dot(mats[1], x_hi)
            y_off = _dot(cm, st.astype(BF16)) * jnp.where(lane_lo, cols[0], cols[1])
            y_ref[0, :, pr * LANES:(pr + 1) * LANES] = (
                y_diag + y_off + xs * dsk_ref[:, pr * LANES:(pr + 1) * LANES])
            inc = _dot(bws[0], x_lo) + _dot(bws[1], x_hi)
            state_s[pr] = st * jnp.where(lane_lo, decs[0], decs[1]) + inc


def _ssd_scan(xd, conv_w, conv_b, dt_bias, a_log, d):
    b, s, _ = xd.shape
    pad = lambda t: jnp.pad(t.astype(F32), (0, SSD_DT_PAD - SSD_HEADS)).reshape(1, SSD_DT_PAD)
    dsk = jnp.repeat(d.astype(F32), SSD_HEAD_DIM).reshape(1, SSD_WIDTH)
    tril = jnp.asarray(np.tril(np.ones((SSD_CHUNK, SSD_CHUNK), np.float32)))
    return pl.pallas_call(
        _ssd_kernel,
        grid=(b, s // SSD_CHUNK),
        in_specs=[pl.BlockSpec((1, SSD_CHUNK, SSD_IN), lambda bi, c: (bi, c, 0)),
                  _const_spec((SSD_CONV, SSD_CONV_DIM)), _const_spec((1, SSD_CONV_DIM)),
                  _const_spec((1, SSD_DT_PAD)), _const_spec((1, SSD_DT_PAD)),
                  _const_spec((1, SSD_WIDTH)), _const_spec((SSD_CHUNK, SSD_CHUNK))],
        out_specs=pl.BlockSpec((1, SSD_CHUNK, SSD_WIDTH), lambda bi, c: (bi, c, 0)),
        out_shape=jax.ShapeDtypeStruct((b, s, SSD_WIDTH), F32),
        scratch_shapes=[pltpu.VMEM((SSD_CHUNK + 8, SSD_CONV_DIM), F32),
                        pltpu.VMEM((SSD_HEADS // 2, SSD_STATE, LANES), F32)],
        compiler_params=pltpu.CompilerParams(dimension_semantics=("parallel", "arbitrary"),
                                             vmem_limit_bytes=VMEM_LIMIT),
        name="ssd_scan",
    )(xd, conv_w.astype(F32), conv_b.astype(F32).reshape(1, SSD_CONV_DIM),
      pad(dt_bias), pad(a_log), dsk, tril)


def _gelu_tanh(x):
    c = math.sqrt(2.0 / math.pi)
    return 0.5 * x * (1.0 + jnp.tanh(c * (x + 0.044715 * (x * x * x))))


def _merge_kernel(x_ref, nw_ref, wz_ref, ya_ref, yb_ref, yc_ref, gw_ref, gb_ref, snw_ref,
                  pa_ref, pb_ref, pc_ref, wo_ref, out_ref):
    x = x_ref[...]
    h = _rms_rows(x, nw_ref[...]).astype(BF16)

    def gate(i):
        lo = Z_GATE + i * D_MODEL
        return _sigmoid(_dot(h, wz_ref[:, lo:lo + D_MODEL]))

    g = _gelu_tanh(jnp.concatenate([ya_ref[c] for c in range(S5_TILES)], axis=1))
    y_a = g * _sigmoid(_dot(g.astype(BF16), gw_ref[...]) + gb_ref[...])
    y_a = y_a * _silu(_dot(h, wz_ref[:, Z_ZA:Z_ZB]))
    merged = gate(0) * _dot(y_a.astype(BF16), pa_ref[...])

    y_b = jnp.concatenate([yb_ref[c] for c in range(ATT_OUT_TILES)], axis=1)
    y_b = y_b * _silu(_dot(h, wz_ref[:, Z_ZB:Z_ZC]))
    merged = merged + gate(1) * _dot(y_b.astype(BF16), pb_ref[...])

    y_c = yc_ref[...] * _silu(_dot(h, wz_ref[:, Z_ZC:Z_GATE]))
    y_c = _rms_rows(y_c, snw_ref[...])
    merged = merged + gate(2) * _dot(y_c.astype(BF16), pc_ref[...])

    out_ref[...] = x + _dot(merged.astype(BF16), wo_ref[...])


def _merge(x2, norm_w, w_z, ya4, yb2, yc, glu_w, glu_b, ssd_nw, pa, pb, pc, wo, batch, seq):
    n = x2.shape[0]
    tm = ROW_TILE
    row = lambda w: pl.BlockSpec((tm, w), lambda i: (i, 0))
    tiled = lambda t: pl.BlockSpec((t, tm, LANES), lambda i: (0, i, 0))
    s5_tiled = pl.BlockSpec(
        (S5_TILES, tm, LANES),
        lambda i: (0, _time_major_block(i, batch, seq, tm, min(S5_TT, seq)), 0))
    return pl.pallas_call(
        _merge_kernel,
        grid=(n // tm,),
        in_specs=[row(D_MODEL), _const_spec((1, D_MODEL)), _const_spec((D_MODEL, Z_WIDTH)),
                  s5_tiled, tiled(ATT_OUT_TILES), row(SSD_WIDTH),
                  _const_spec((S5_WIDTH, S5_WIDTH)), _const_spec((1, S5_WIDTH)),
                  _const_spec((1, SSD_WIDTH)), _const_spec((S5_WIDTH, D_MODEL)),
                  _const_spec((ATT_GW, D_MODEL)), _const_spec((SSD_WIDTH, D_MODEL)),
                  _const_spec((D_MODEL, D_MODEL))],
        out_specs=row(D_MODEL),
        out_shape=jax.ShapeDtypeStruct((n, D_MODEL), F32),
        compiler_params=pltpu.CompilerParams(dimension_semantics=("parallel",),
                                             vmem_limit_bytes=VMEM_LIMIT),
        name="merge",
    )(x2, norm_w, w_z, ya4, yb2, yc, glu_w, glu_b, ssd_nw, pa, pb, pc, wo)


def _layer(x, norm_w, w_in, s5_a_re, s5_a_im, s5_log_step, s5_b_re, s5_b_im, s5_c_re,
           s5_c_im, s5_d, s5_glu_w, s5_glu_b, q_norm_w, k_norm_w, conv_w, conv_b,
           dt_bias, ssd_a_log, ssd_d, ssd_norm_w, proj_a, proj_b, proj_c, w_out):
    b, s, _ = x.shape
    n = b * s
    x2 = x.reshape(n, D_MODEL)
    nw = norm_w.astype(F32).reshape(1, D_MODEL)
    w_in = w_in.astype(F32)
    w_a = jnp.concatenate(
        [w_in[:, O_UA:O_ZA], w_in[:, O_Q:O_ZB], w_in[:, O_XBC:O_ZC],
         jnp.zeros((D_MODEL, SSD_DT_PAD - SSD_HEADS), F32)], axis=1).astype(BF16)
    w_z = jnp.concatenate(
        [w_in[:, O_ZA:O_Q], w_in[:, O_ZB:O_XBC], w_in[:, O_ZC:O_END]], axis=1).astype(BF16)
    qw = jnp.tile(q_norm_w.astype(F32) * (ATT_HEAD_DIM ** -0.5), ATT_WIDTH // ATT_HEAD_DIM)
    kw = jnp.tile(k_norm_w.astype(F32), ATT_WIDTH // ATT_HEAD_DIM)
    blk = np.arange(ATT_GW) // ATT_HEAD_DIM
    ones = jnp.asarray((blk[:, None] == blk[None, :]).astype(np.float32) / ATT_HEAD_DIM, BF16)

    u4, q6, k6, v6, xd = _inproj(x2, nw, w_a, qw.reshape(1, -1), kw.reshape(1, -1), ones, b, s)

    s5w = _s5_weights(s5_a_re, s5_a_im, s5_log_step, s5_b_re, s5_b_im, s5_c_re, s5_c_im, s5_d)
    ya4 = _s5_scan(u4, s5w, b, s)

    tile4 = lambda t: t.reshape(ATT_TILES, b, s, LANES)
    yb2 = _attention(tile4(q6), tile4(k6), tile4(v6)).reshape(ATT_OUT_TILES, n, LANES)

    y_c = _ssd_scan(xd.reshape(b, s, SSD_IN), conv_w, conv_b, dt_bias, ssd_a_log, ssd_d)

    out = _merge(x2, nw, w_z, ya4, yb2, y_c.reshape(n, SSD_WIDTH),
                 s5_glu_w.astype(BF16), s5_glu_b.astype(F32).reshape(1, S5_WIDTH),
                 ssd_norm_w.astype(F32).reshape(1, SSD_WIDTH),
                 proj_a.astype(BF16), proj_b.astype(BF16), proj_c.astype(BF16),
                 w_out.astype(BF16), b, s)
    return out.reshape(b, s, D_MODEL)


def kernel(x, norm_w, w_in, s5_a_re, s5_a_im, s5_log_step, s5_b_re, s5_b_im, s5_c_re, s5_c_im,
           s5_d, s5_glu_w, s5_glu_b, q_norm_w, k_norm_w, conv_w, conv_b, dt_bias, ssd_a_log,
           ssd_d, ssd_norm_w, proj_a, proj_b, proj_c, w_out):
    params = (norm_w, w_in, s5_a_re, s5_a_im, s5_log_step, s5_b_re, s5_b_im, s5_c_re, s5_c_im,
              s5_d, s5_glu_w, s5_glu_b, q_norm_w, k_norm_w, conv_w, conv_b, dt_bias, ssd_a_log,
              ssd_d, ssd_norm_w, proj_a, proj_b, proj_c, w_out)
    for i in range(norm_w.shape[0]):
        x = _layer(x, *(p[i] for p in params))
    return x
```

```python
import functools
import math

import numpy as np
import jax
import jax.numpy as jnp
from jax import lax
from jax.experimental import pallas as pl
from jax.experimental.pallas import tpu as pltpu

D_MODEL = 1024
RMS_EPS = 1e-6
LANES = 128

S5_WIDTH = 512
S5_GROUP = 16
S5_GROUPS = 32
S5_STATE = 64
S5_CHUNK = 16
S5_PAIRS = S5_GROUPS // 2
S5_TILES = S5_WIDTH // LANES
S5_PPT = S5_PAIRS // S5_TILES
S5_PW = 2 * S5_GROUP
S5_TT = 512

ATT_HEAD_DIM = 64
ATT_PAIRS = ((128, 1), (512, 4), (2048, 16))
ATT_HPG = 4
ATT_WIDTH = 768
ATT_GW = ATT_HPG * ATT_HEAD_DIM
ATT_BLOCK = 128
ATT_TILES = ATT_WIDTH // LANES
ATT_OUT_TILES = ATT_GW // LANES
ATT_TT = 2048

SSD_HEAD_DIM = 64
SSD_WIDTH = 768
SSD_HEADS = 12
SSD_GROUPS = 2
SSD_STATE = 128
SSD_CONV = 4
SSD_CHUNK = 128
SSD_CONV_DIM = 1280
SSD_DT_PAD = 128
SSD_IN = SSD_CONV_DIM + SSD_DT_PAD

IN_SPLITS = (512, 512, 768, 768, 768, 256, 1280, 12, 768, 3072)
_OFF = np.concatenate([[0], np.cumsum(IN_SPLITS)]).tolist()
(O_UA, O_ZA, O_Q, O_K, O_V, O_ZB, O_XBC, O_DT, O_ZC, O_GATE, O_END) = _OFF

A_UA, A_Q, A_K, A_V, A_XBC = 0, 512, 1280, 2048, 2816
A_WIDTH = A_XBC + SSD_IN
Z_ZA, Z_ZB, Z_ZC, Z_GATE = 0, 512, 768, 1536
Z_WIDTH = Z_GATE + 3 * D_MODEL

ROW_TILE = 256
VMEM_LIMIT = 56 * 1024 * 1024

BF16 = jnp.bfloat16
F32 = jnp.float32


def _dot(a, b):
    return jnp.dot(a, b, preferred_element_type=F32)


def _dot_nt(a, b):
    return lax.dot_general(a, b, (((1,), (1,)), ((), ())), preferred_element_type=F32)


def _const_spec(shape):
    nd = len(shape)
    return pl.BlockSpec(shape, lambda *_: (0,) * nd, pipeline_mode=pl.Buffered(1))


def _sigmoid(x):
    return 1.0 / (1.0 + jnp.exp(-x))


def _silu(x):
    return x * _sigmoid(x)


def _rms_rows(x, w):
    return x * lax.rsqrt(jnp.mean(x * x, axis=-1, keepdims=True) + RMS_EPS) * w


def _inproj_kernel(x_ref, nw_ref, w_ref, qw_ref, kw_ref, ones_ref,
                   ua_ref, q_ref, k_ref, v_ref, xd_ref):
    h = _rms_rows(x_ref[...], nw_ref[...]).astype(BF16)
    for c in range(0, S5_TILES, 2):
        t = _dot(h, w_ref[:, A_UA + c * LANES:A_UA + (c + 2) * LANES])
        ua_ref[c] = t[:, :LANES]
        ua_ref[c + 1] = t[:, LANES:]
    ones = ones_ref[...]

    def normed(col, gain):
        t = _dot(h, w_ref[:, col:col + ATT_GW])
        ms = _dot((t * t).astype(BF16), ones)
        return t * lax.rsqrt(ms + RMS_EPS) * gain

    for j in range(ATT_WIDTH // ATT_GW):
        lo = j * ATT_GW
        qj = normed(A_Q + lo, qw_ref[:, lo:lo + ATT_GW])
        kj = normed(A_K + lo, kw_ref[:, lo:lo + ATT_GW])
        vj = _dot(h, w_ref[:, A_V + lo:A_V + lo + ATT_GW])
        for half in range(ATT_GW // LANES):
            sl = slice(half * LANES, (half + 1) * LANES)
            q_ref[2 * j + half] = qj[:, sl]
            k_ref[2 * j + half] = kj[:, sl]
            v_ref[2 * j + half] = vj[:, sl]
    for lo in range(0, SSD_IN, 2 * LANES):
        hi = min(lo + 2 * LANES, SSD_IN)
        xd_ref[:, lo:hi] = _dot(h, w_ref[:, A_XBC + lo:A_XBC + hi])


def _time_major_block(i, batch, seq, tm, tt):
    per_seq = seq // tm
    b, r = i // per_seq, i % per_seq
    sub = tt // tm
    return ((r // sub) * batch + b) * sub + r % sub


def _inproj(x2, norm_w, w_a, qw, kw, ones, batch, seq):
    n = x2.shape[0]
    tm = ROW_TILE
    row = lambda w: pl.BlockSpec((tm, w), lambda i: (i, 0))
    tiled = lambda t: pl.BlockSpec((t, tm, LANES), lambda i: (0, i, 0))
    s5_tiled = pl.BlockSpec(
        (S5_TILES, tm, LANES),
        lambda i: (0, _time_major_block(i, batch, seq, tm, min(S5_TT, seq)), 0))
    return pl.pallas_call(
        _inproj_kernel,
        grid=(n // tm,),
        in_specs=[row(D_MODEL), _const_spec((1, D_MODEL)), _const_spec((D_MODEL, A_WIDTH)),
                  _const_spec((1, ATT_WIDTH)), _const_spec((1, ATT_WIDTH)),
                  _const_spec((ATT_GW, ATT_GW))],
        out_specs=[s5_tiled, tiled(ATT_TILES), tiled(ATT_TILES), tiled(ATT_TILES), row(SSD_IN)],
        out_shape=[jax.ShapeDtypeStruct((S5_TILES, n, LANES), F32),
                   jax.ShapeDtypeStruct((ATT_TILES, n, LANES), F32),
                   jax.ShapeDtypeStruct((ATT_TILES, n, LANES), F32),
                   jax.ShapeDtypeStruct((ATT_TILES, n, LANES), F32),
                   jax.ShapeDtypeStruct((n, SSD_IN), F32)],
        compiler_params=pltpu.CompilerParams(dimension_semantics=("parallel",),
                                             vmem_limit_bytes=VMEM_LIMIT),
        name="inproj",
    )(x2, norm_w, w_a, qw, kw, ones)


def _s5_weights(a_re, a_im, log_step, b_re, b_im, c_re, c_im, d):
    G, P, I, T = S5_GROUPS, S5_STATE, S5_GROUP, S5_CHUNK
    a_re, a_im = a_re.astype(F32), a_im.astype(F32)
    b_re, b_im = b_re.astype(F32), b_im.astype(F32)
    c_re, c_im = c_re.astype(F32), c_im.astype(F32)
    step = jnp.exp(log_step.astype(F32))[:, None]
    mag = jnp.exp(a_re * step)
    ang = a_im * step
    lam_re, lam_im = mag * jnp.cos(ang), mag * jnp.sin(ang)
    num_re, num_im = lam_re - 1.0, lam_im
    den = a_re * a_re + a_im * a_im
    f_re = (num_re * a_re + num_im * a_im) / den
    f_im = (num_im * a_re - num_re * a_im) / den
    bb_re = f_re[..., None] * b_re - f_im[..., None] * b_im
    bb_im = f_re[..., None] * b_im + f_im[..., None] * b_re
    pw_re, pw_im = [jnp.ones_like(lam_re)], [jnp.zeros_like(lam_im)]
    for _ in range(T):
        r, i = pw_re[-1], pw_im[-1]
        pw_re.append(r * lam_re - i * lam_im)
        pw_im.append(r * lam_im + i * lam_re)
    pw_re, pw_im = jnp.stack(pw_re), jnp.stack(pw_im)
    hp = lax.Precision.HIGHEST
    ps_re, ps_im = pw_re[T - 1::-1], pw_im[T - 1::-1]
    wst_re = (jnp.einsum('sgp,gpi->gsip', ps_re, bb_re, precision=hp)
              - jnp.einsum('sgp,gpi->gsip', ps_im, bb_im, precision=hp))
    wst_im = (jnp.einsum('sgp,gpi->gsip', ps_re, bb_im, precision=hp)
              + jnp.einsum('sgp,gpi->gsip', ps_im, bb_re, precision=hp))
    po_re, po_im = pw_re[1:T + 1], pw_im[1:T + 1]
    wo_re = (jnp.einsum('gip,tgp->gpti', c_re, po_re, precision=hp)
             - jnp.einsum('gip,tgp->gpti', c_im, po_im, precision=hp))
    wo_im = -(jnp.einsum('gip,tgp->gpti', c_re, po_im, precision=hp)
              + jnp.einsum('gip,tgp->gpti', c_im, po_re, precision=hp))
    cl_re = c_re[None] * pw_re[:T, :, None, :] - c_im[None] * pw_im[:T, :, None, :]
    cl_im = c_re[None] * pw_im[:T, :, None, :] + c_im[None] * pw_re[:T, :, None, :]
    kj = (jnp.einsum('jgop,gpi->jgoi', cl_re, bb_re, precision=hp)
          - jnp.einsum('jgop,gpi->jgoi', cl_im, bb_im, precision=hp))
    kj = kj.at[0].add(d.astype(F32).reshape(G, I)[:, :, None] * jnp.eye(I, dtype=F32)[None])

    eye2 = jnp.eye(2, dtype=F32)
    pairs = lambda w: w.reshape((S5_PAIRS, 2) + w.shape[1:])
    rows = T * S5_PW
    wst = jnp.concatenate(
        [jnp.einsum('rgsip,gh->rsgihp', pairs(w), eye2).reshape(S5_PAIRS, rows, 2 * P)
         for w in (wst_re, wst_im)], axis=2)
    wo = jnp.concatenate(
        [jnp.einsum('rgpto,gh->rgptho', pairs(w), eye2).reshape(S5_PAIRS, 2 * P, rows)
         for w in (wo_re, wo_im)], axis=1)
    brow = jnp.einsum('jrgoi,gh->rgijho', kj.reshape(T, S5_PAIRS, 2, I, I), eye2)
    brow = brow.reshape(S5_PAIRS, S5_PW, rows).astype(BF16)
    src = lax.broadcasted_iota(jnp.int32, (T, rows, rows), 1)
    dst = lax.broadcasted_iota(jnp.int32, (T, rows, rows), 2)
    s_ix = lax.broadcasted_iota(jnp.int32, (T, rows, rows), 0)
    shift = jnp.logical_and(src % S5_PW == dst % S5_PW,
                            dst // S5_PW == s_ix + src // S5_PW).astype(BF16)
    wtoep = jnp.einsum('rac,scd->rsad', brow, shift,
                       preferred_element_type=F32).reshape(S5_PAIRS, rows, rows)
    lam = jnp.stack([pw_re[T].reshape(S5_PAIRS, 2 * P), pw_im[T].reshape(S5_PAIRS, 2 * P)],
                    axis=1)
    return wst.astype(BF16), wtoep.astype(BF16), wo.astype(BF16), lam


def _s5_perm():
    n = S5_PPT * LANES
    a = np.arange(n)
    k, pp, j = a // LANES, (a % LANES) // S5_PW, a % S5_PW
    p = np.zeros((n, n), np.float32)
    p[a, pp * LANES + k * S5_PW + j] = 1.0
    return p


def _s5_kernel(u_ref, wst_ref, wt_ref, wo_ref, lam_ref, perm_ref, permt_ref, y_ref,
               x_s, uc_s, s_s, h_s, cre_s, cim_s, *, batch, tt):
    nch = tt // S5_CHUNK
    quads = S5_CHUNK * S5_PW // LANES
    per_q = LANES // S5_PW

    @pl.when(pl.program_id(1) == 0)
    def _():
        cre_s[...] = jnp.zeros_like(cre_s)
        cim_s[...] = jnp.zeros_like(cim_s)

    u2 = u_ref.at[0]
    y2 = y_ref.at[0]

    def gather(n, carry):
        r0 = pl.multiple_of(n * batch, batch)
        for q in range(quads):
            for k in range(per_q):
                x_s[q, pl.ds(r0, batch), k * LANES:(k + 1) * LANES] = (
                    u2[pl.ds(n * S5_CHUNK + q * per_q + k, batch, stride=tt), :])
        return carry

    lax.fori_loop(0, nch, gather, 0)

    perm = perm_ref[...]
    for q in range(quads):
        ucq = _dot(x_s[q].astype(BF16), perm)
        for pp in range(S5_PPT):
            uc_s[pp, :, q * LANES:(q + 1) * LANES] = ucq[:, pp * LANES:(pp + 1) * LANES].astype(BF16)

    for pp in range(S5_PPT):
        u = uc_s[pp]
        s_s[...] = _dot(u, wst_ref[pp])
        lre = jnp.broadcast_to(lam_ref[pp, 0:1, :], (batch, LANES))
        lim = jnp.broadcast_to(lam_ref[pp, 1:2, :], (batch, LANES))

        def scan(n, carry):
            hr, hi = carry
            r0 = pl.multiple_of(n * batch, batch)
            h_s[pl.ds(r0, batch), 0:LANES] = hr
            h_s[pl.ds(r0, batch), LANES:2 * LANES] = hi
            sr = s_s[pl.ds(r0, batch), 0:LANES]
            si = s_s[pl.ds(r0, batch), LANES:2 * LANES]
            return (lre * hr - lim * hi + sr, lre * hi + lim * hr + si)

        hr, hi = lax.fori_loop(0, nch, scan, (cre_s[pp], cim_s[pp]))
        cre_s[pp] = hr
        cim_s[pp] = hi
        y = _dot(u, wt_ref[pp]) + _dot(h_s[...].astype(BF16), wo_ref[pp])
        for q in range(quads):
            x_s[q, :, pp * LANES:(pp + 1) * LANES] = y[:, q * LANES:(q + 1) * LANES]

    permt = permt_ref[...]
    for q in range(quads):
        yq = x_s[q]
        hi = yq.astype(BF16)
        lo = (yq - hi.astype(F32)).astype(BF16)
        x_s[q] = _dot(hi, permt) + _dot(lo, permt)

    def scatter(n, carry):
        r0 = pl.multiple_of(n * batch, batch)
        for q in range(quads):
            for k in range(per_q):
                y2[pl.ds(n * S5_CHUNK + q * per_q + k, batch, stride=tt), :] = (
                    x_s[q, pl.ds(r0, batch), k * LANES:(k + 1) * LANES])
        return carry

    lax.fori_loop(0, nch, scatter, 0)


def _s5_scan(u4, weights, b, s):
    wst, wt, wo, lam = weights
    tt = min(S5_TT, s)
    rows = tt // S5_CHUNK * b
    width = S5_CHUNK * S5_PW
    perm = _s5_perm()
    act = pl.BlockSpec((1, b * tt, LANES), lambda c, j: (c, j, 0))
    wspec = lambda a_, b_: pl.BlockSpec((S5_PPT, a_, b_), lambda c, j: (c, 0, 0))
    return pl.pallas_call(
        functools.partial(_s5_kernel, batch=b, tt=tt),
        grid=(S5_TILES, s // tt),
        in_specs=[act, wspec(width, 2 * LANES), wspec(width, width), wspec(2 * LANES, width),
                  wspec(2, LANES), _const_spec((width, width)), _const_spec((width, width))],
        out_specs=act,
        out_shape=jax.ShapeDtypeStruct(u4.shape, F32),
        scratch_shapes=[pltpu.VMEM((S5_PPT, rows, width), F32),
                        pltpu.VMEM((S5_PPT, rows, width), BF16),
                        pltpu.VMEM((rows, 2 * LANES), F32), pltpu.VMEM((rows, 2 * LANES), F32),
                        pltpu.VMEM((S5_PPT, b, LANES), F32), pltpu.VMEM((S5_PPT, b, LANES), F32)],
        compiler_params=pltpu.CompilerParams(dimension_semantics=("parallel", "arbitrary"),
                                             vmem_limit_bytes=VMEM_LIMIT),
        name="s5_scan",
    )(u4, wst, wt, wo, lam, jnp.asarray(perm, BF16), jnp.asarray(perm.T, BF16))


def _attn_kernel(q_ref, k_ref, v_ref, ones_ref, o_ref,
                 kring, vring, sc0, sp0, sc1, sp1, m_s, l_s, n_s, *, tt):
    tile = pl.program_id(1)
    step = pl.program_id(2)
    ng = len(ATT_PAIRS)
    group = step % ng
    src = group * ATT_OUT_TILES + step // ng
    qv, kv, vv, ov = q_ref.at[0, 0], k_ref.at[0, 0], v_ref.at[0, 0], o_ref.at[0, 0]
    nblk = tt // ATT_BLOCK
    ring = 2 * nblk
    half = (tile % 2) * nblk

    @pl.when(tile == 0)
    def _():
        zero = jnp.zeros((nblk, ATT_BLOCK, LANES), BF16)
        kring[src, nblk:ring] = zero
        vring[src, nblk:ring] = zero

    lane_lo = lax.broadcasted_iota(jnp.int32, (1, LANES), 1) < ATT_HEAD_DIM
    row = lax.broadcasted_iota(jnp.int32, (2 * ATT_BLOCK, ATT_BLOCK), 0) % ATT_BLOCK
    col = lax.broadcasted_iota(jnp.int32, (2 * ATT_BLOCK, ATT_BLOCK), 1)
    mask_cur = col <= row
    mask_prev = col >= row
    neg = jnp.float32(-jnp.inf)
    ones = ones_ref[...]

    def halves(t):
        return jnp.where(lane_lo, t[:ATT_BLOCK], t[ATT_BLOCK:])

    def rows_of(i, dil):
        return pl.ds((i // dil) * (ATT_BLOCK * dil) + i % dil, ATT_BLOCK, stride=dil)

    def stage_a(i, dil, sc_buf, sp_buf):
        rows = rows_of(i, dil)
        q = qv[rows, :]
        kc = kv[rows, :].astype(BF16)
        slot = half + i
        kring[src, slot] = kc
        vring[src, slot] = vv[rows, :].astype(BF16)
        kp = kring[src, (slot + ring - dil) % ring]
        has_prev = jnp.logical_or(tile > 0, i >= dil)
        q2 = jnp.concatenate([jnp.where(lane_lo, q, 0.0), jnp.where(lane_lo, 0.0, q)],
                             axis=0).astype(BF16)
        sc_buf[...] = jnp.where(mask_cur, _dot_nt(q2, kc), neg)
        sp_buf[...] = jnp.where(jnp.logical_and(mask_prev, has_prev), _dot_nt(q2, kp), neg)

    def stage_b(i, dil, gi, sc_buf, sp_buf):
        rows = rows_of(i, dil)
        slot = half + i
        sc, sp = sc_buf[...], sp_buf[...]
        m = jnp.max(jnp.maximum(sc, sp), axis=-1, keepdims=True)
        pc = jnp.exp2(sc - m).astype(BF16)
        pp = jnp.exp2(sp - m).astype(BF16)
        vc = jnp.concatenate([vring[src, slot], ones], axis=1)
        vp = jnp.concatenate([vring[src, (slot + ring - dil) % ring], ones], axis=1)
        r = _dot(pc, vc) + _dot(pp, vp)
        acc = halves(r[:, :LANES])
        lb = halves(r[:, LANES:])
        mb = halves(jnp.broadcast_to(m, (2 * ATT_BLOCK, LANES)))
        if gi > 0:
            m_old, l_old, n_old = m_s[rows, :], l_s[rows, :], n_s[rows, :]
            m_new = jnp.maximum(m_old, mb)
            a_old = jnp.exp2(m_old - m_new)
            a_new = jnp.exp2(mb - m_new)
            acc = n_old * a_old + acc * a_new
            lb = l_old * a_old + lb * a_new
            mb = m_new
        if gi == ng - 1:
            ov[rows, :] = acc / lb
        else:
            m_s[rows, :] = mb
            l_s[rows, :] = lb
            n_s[rows, :] = acc

    for gi, (_, dil) in enumerate(ATT_PAIRS):
        @pl.when(group == gi)
        def _(gi=gi, dil=dil):
            stage_a(0, dil, sc0, sp0)

            def body(ii, c):
                i0 = 2 * ii
                stage_b(i0, dil, gi, sc0, sp0)
                stage_a(i0 + 1, dil, sc1, sp1)
                stage_b(i0 + 1, dil, gi, sc1, sp1)
                stage_a(jnp.minimum(i0 + 2, nblk - 1), dil, sc0, sp0)
                return c

            lax.fori_loop(0, nblk // 2, body, 0)


def _attention(q6, k6, v6):
    _, b, s, _ = q6.shape
    tt = min(ATT_TT, s)
    nblk = tt // ATT_BLOCK
    ng = len(ATT_PAIRS)
    src = lambda i: (i % ng) * ATT_OUT_TILES + i // ng
    blk = (1, 1, tt, LANES)
    cur = pl.BlockSpec(blk, lambda bi, t, i: (src(i), bi, t, 0))
    out = pl.BlockSpec(blk, lambda bi, t, i: (i // ng, bi, t, 0))
    ones = jnp.ones((ATT_BLOCK, LANES), BF16)
    score = pltpu.VMEM((2 * ATT_BLOCK, ATT_BLOCK), F32)
    return pl.pallas_call(
        functools.partial(_attn_kernel, tt=tt),
        grid=(b, s // tt, ATT_TILES),
        in_specs=[cur, cur, cur, _const_spec((ATT_BLOCK, LANES))],
        out_specs=out,
        out_shape=jax.ShapeDtypeStruct((ATT_OUT_TILES, b, s, LANES), F32),
        scratch_shapes=[pltpu.VMEM((ATT_TILES, 2 * nblk, ATT_BLOCK, LANES), BF16)] * 2
        + [score] * 4 + [pltpu.VMEM((tt, LANES), F32)] * 3,
        compiler_params=pltpu.CompilerParams(
            dimension_semantics=("parallel", "arbitrary", "arbitrary"),
            vmem_limit_bytes=VMEM_LIMIT),
        name="attention",
    )(q6, k6, v6, ones)


def _ssd_kernel(xd_ref, cw_ref, cb_ref, dtb_ref, alog_ref, dsk_ref, tril_ref, y_ref,
                xpad_s, state_s):
    L = SSD_CHUNK

    @pl.when(pl.program_id(1) == 0)
    def _():
        xpad_s[0:8, :] = jnp.zeros((8, SSD_CONV_DIM), F32)
        state_s[...] = jnp.zeros_like(state_s)

    xpad_s[8:8 + L, :] = xd_ref[0, :, 0:SSD_CONV_DIM]
    xe = xpad_s[...]
    acc = cb_ref[...] + cw_ref[SSD_CONV - 1:SSD_CONV, :] * xe[8:8 + L]
    for back in range(1, SSD_CONV):
        kk = SSD_CONV - 1 - back
        acc = acc + cw_ref[kk:kk + 1, :] * pltpu.roll(xe, back, axis=0)[8:8 + L]
    xpad_s[0:8, :] = xe[L:L + 8]
    xc = _silu(acc)

    lane = lax.broadcasted_iota(jnp.int32, (1, LANES), 1)
    dtr = xd_ref[0, :, SSD_CONV_DIM:SSD_IN] + dtb_ref[...]
    dt = jnp.maximum(dtr, 0.0) + jnp.log(1.0 + jnp.exp(-jnp.abs(dtr)))
    a = jnp.where(lane < SSD_HEADS, -jnp.exp(alog_ref[...]) * math.log2(math.e), 0.0)
    a_dt = dt * a
    a_cs = jnp.dot(tril_ref[...], a_dt, preferred_element_type=F32,
                   precision=lax.Precision.HIGHEST)
    ea = jnp.exp2(a_cs)
    nhp = 16
    last_t = a_cs.T[0:nhp, L - 1:L]
    r_t = (a_cs - jnp.log2(dt)).T[0:nhp]
    w_t = jnp.exp2(last_t - r_t)
    el_t = jnp.exp2(last_t)

    row = lax.broadcasted_iota(jnp.int32, (L, L), 0)
    col = lax.broadcasted_iota(jnp.int32, (L, L), 1)
    tri = row >= col
    lane_lo = lane < SSD_HEAD_DIM
    neg = jnp.float32(-jnp.inf)

    for g in range(SSD_GROUPS):
        bm = xc[:, SSD_WIDTH + g * SSD_STATE:SSD_WIDTH + (g + 1) * SSD_STATE]
        cm = xc[:, SSD_WIDTH + (SSD_GROUPS + g) * SSD_STATE:
                SSD_WIDTH + (SSD_GROUPS + g + 1) * SSD_STATE].astype(BF16)
        cb = _dot_nt(cm, bm.astype(BF16))
        bm_t = bm.T
        for pr in range(3 * g, 3 * g + 3):
            xs = xc[:, pr * LANES:(pr + 1) * LANES]
            x_lo = jnp.where(lane_lo, xs, 0.0).astype(BF16)
            x_hi = jnp.where(lane_lo, 0.0, xs).astype(BF16)
            mats, bws, cols, decs = [], [], [], []
            for h in (2 * pr, 2 * pr + 1):
                c_col = a_cs[:, h:h + 1]
                dec = jnp.exp2(jnp.where(tri, c_col - r_t[h:h + 1, :], neg))
                mats.append((cb * dec).astype(BF16))
                bws.append((bm_t * w_t[h:h + 1, :]).astype(BF16))
                cols.append(ea[:, h:h + 1])
                decs.append(el_t[h:h + 1, :])
            st = state_s[pr]
            y_diag = _dot(mats[0], x_lo) + _dot(mats[1], x_hi)
            y_off = _dot(cm, st.astype(BF16)) * jnp.where(lane_lo, cols[0], cols[1])
            y_ref[0, :, pr * LANES:(pr + 1) * LANES] = (
                y_diag + y_off + xs * dsk_ref[:, pr * LANES:(pr + 1) * LANES])
            inc = _dot(bws[0], x_lo) + _dot(bws[1], x_hi)
            state_s[pr] = st * jnp.where(lane_lo, decs[0], decs[1]) + inc


def _ssd_scan(xd, conv_w, conv_b, dt_bias, a_log, d):
    b, s, _ = xd.shape
    pad = lambda t: jnp.pad(t.astype(F32), (0, SSD_DT_PAD - SSD_HEADS)).reshape(1, SSD_DT_PAD)
    dsk = jnp.repeat(d.astype(F32), SSD_HEAD_DIM).reshape(1, SSD_WIDTH)
    tril = jnp.asarray(np.tril(np.ones((SSD_CHUNK, SSD_CHUNK), np.float32)))
    return pl.pallas_call(
        _ssd_kernel,
        grid=(b, s // SSD_CHUNK),
        in_specs=[pl.BlockSpec((1, SSD_CHUNK, SSD_IN), lambda bi, c: (bi, c, 0)),
                  _const_spec((SSD_CONV, SSD_CONV_DIM)), _const_spec((1, SSD_CONV_DIM)),
                  _const_spec((1, SSD_DT_PAD)), _const_spec((1, SSD_DT_PAD)),
                  _const_spec((1, SSD_WIDTH)), _const_spec((SSD_CHUNK, SSD_CHUNK))],
        out_specs=pl.BlockSpec((1, SSD_CHUNK, SSD_WIDTH), lambda bi, c: (bi, c, 0)),
        out_shape=jax.ShapeDtypeStruct((b, s, SSD_WIDTH), F32),
        scratch_shapes=[pltpu.VMEM((SSD_CHUNK + 8, SSD_CONV_DIM), F32),
                        pltpu.VMEM((SSD_HEADS // 2, SSD_STATE, LANES), F32)],
        compiler_params=pltpu.CompilerParams(dimension_semantics=("parallel", "arbitrary"),
                                             vmem_limit_bytes=VMEM_LIMIT),
        name="ssd_scan",
    )(xd, conv_w.astype(F32), conv_b.astype(F32).reshape(1, SSD_CONV_DIM),
      pad(dt_bias), pad(a_log), dsk, tril)


def _gelu_tanh(x):
    c = math.sqrt(2.0 / math.pi)
    return 0.5 * x * (1.0 + jnp.tanh(c * (x + 0.044715 * (x * x * x))))


def _merge_kernel(x_ref, nw_ref, wz_ref, ya_ref, yb_ref, yc_ref, gw_ref, gb_ref, snw_ref,
                  pa_ref, pb_ref, pc_ref, wo_ref, out_ref):
    x = x_ref[...]
    h = _rms_rows(x, nw_ref[...]).astype(BF16)

    def gate(i):
        lo = Z_GATE + i * D_MODEL
        return _sigmoid(_dot(h, wz_ref[:, lo:lo + D_MODEL]))

    g = _gelu_tanh(jnp.concatenate([ya_ref[c] for c in range(S5_TILES)], axis=1))
    y_a = g * _sigmoid(_dot(g.astype(BF16), gw_ref[...]) + gb_ref[...])
    y_a = y_a * _silu(_dot(h, wz_ref[:, Z_ZA:Z_ZB]))
    merged = gate(0) * _dot(y_a.astype(BF16), pa_ref[...])

    y_b = jnp.concatenate([yb_ref[c] for c in range(ATT_OUT_TILES)], axis=1)
    y_b = y_b * _silu(_dot(h, wz_ref[:, Z_ZB:Z_ZC]))
    merged = merged + gate(1) * _dot(y_b.astype(BF16), pb_ref[...])

    y_c = yc_ref[...] * _silu(_dot(h, wz_ref[:, Z_ZC:Z_GATE]))
    y_c = _rms_rows(y_c, snw_ref[...])
    merged = merged + gate(2) * _dot(y_c.astype(BF16), pc_ref[...])

    out_ref[...] = x + _dot(merged.astype(BF16), wo_ref[...])


def _merge(x2, norm_w, w_z, ya4, yb2, yc, glu_w, glu_b, ssd_nw, pa, pb, pc, wo, batch, seq):
    n = x2.shape[0]
    tm = ROW_TILE
    row = lambda w: pl.BlockSpec((tm, w), lambda i: (i, 0))
    tiled = lambda t: pl.BlockSpec((t, tm, LANES), lambda i: (0, i, 0))
    s5_tiled = pl.BlockSpec(
        (S5_TILES, tm, LANES),
        lambda i: (0, _time_major_block(i, batch, seq, tm, min(S5_TT, seq)), 0))
    return pl.pallas_call(
        _merge_kernel,
        grid=(n // tm,),
        in_specs=[row(D_MODEL), _const_spec((1, D_MODEL)), _const_spec((D_MODEL, Z_WIDTH)),
                  s5_tiled, tiled(ATT_OUT_TILES), row(SSD_WIDTH),
                  _const_spec((S5_WIDTH, S5_WIDTH)), _const_spec((1, S5_WIDTH)),
                  _const_spec((1, SSD_WIDTH)), _const_spec((S5_WIDTH, D_MODEL)),
                  _const_spec((ATT_GW, D_MODEL)), _const_spec((SSD_WIDTH, D_MODEL)),
                  _const_spec((D_MODEL, D_MODEL))],
        out_specs=row(D_MODEL),
        out_shape=jax.ShapeDtypeStruct((n, D_MODEL), F32),
        compiler_params=pltpu.CompilerParams(dimension_semantics=("parallel",),
                                             vmem_limit_bytes=VMEM_LIMIT),
        name="merge",
    )(x2, norm_w, w_z, ya4, yb2, yc, glu_w, glu_b, ssd_nw, pa, pb, pc, wo)


def _layer(x, norm_w, w_in, s5_a_re, s5_a_im, s5_log_step, s5_b_re, s5_b_im, s5_c_re,
           s5_c_im, s5_d, s5_glu_w, s5_glu_b, q_norm_w, k_norm_w, conv_w, conv_b,
           dt_bias, ssd_a_log, ssd_d, ssd_norm_w, proj_a, proj_b, proj_c, w_out):
    b, s, _ = x.shape
    n = b * s
    x2 = x.reshape(n, D_MODEL)
    nw = norm_w.astype(F32).reshape(1, D_MODEL)
    w_in = w_in.astype(F32)
    w_a = jnp.concatenate(
        [w_in[:, O_UA:O_ZA], w_in[:, O_Q:O_ZB], w_in[:, O_XBC:O_ZC],
         jnp.zeros((D_MODEL, SSD_DT_PAD - SSD_HEADS), F32)], axis=1).astype(BF16)
    w_z = jnp.concatenate(
        [w_in[:, O_ZA:O_Q], w_in[:, O_ZB:O_XBC], w_in[:, O_ZC:O_END]], axis=1).astype(BF16)
    qw = jnp.tile(q_norm_w.astype(F32) * (ATT_HEAD_DIM ** -0.5 * math.log2(math.e)),
                  ATT_WIDTH // ATT_HEAD_DIM)
    kw = jnp.tile(k_norm_w.astype(F32), ATT_WIDTH // ATT_HEAD_DIM)
    blk = np.arange(ATT_GW) // ATT_HEAD_DIM
    ones = jnp.asarray((blk[:, None] == blk[None, :]).astype(np.float32) / ATT_HEAD_DIM, BF16)

    u4, q6, k6, v6, xd = _inproj(x2, nw, w_a, qw.reshape(1, -1), kw.reshape(1, -1), ones, b, s)

    s5w = _s5_weights(s5_a_re, s5_a_im, s5_log_step, s5_b_re, s5_b_im, s5_c_re, s5_c_im, s5_d)
    ya4 = _s5_scan(u4, s5w, b, s)

    tile4 = lambda t: t.reshape(ATT_TILES, b, s, LANES)
    yb2 = _attention(tile4(q6), tile4(k6), tile4(v6)).reshape(ATT_OUT_TILES, n, LANES)

    y_c = _ssd_scan(xd.reshape(b, s, SSD_IN), conv_w, conv_b, dt_bias, ssd_a_log, ssd_d)

    out = _merge(x2, nw, w_z, ya4, yb2, y_c.reshape(n, SSD_WIDTH),
                 s5_glu_w.astype(BF16), s5_glu_b.astype(F32).reshape(1, S5_WIDTH),
                 ssd_norm_w.astype(F32).reshape(1, SSD_WIDTH),
                 proj_a.astype(BF16), proj_b.astype(BF16), proj_c.astype(BF16),
                 w_out.astype(BF16), b, s)
    return out.reshape(b, s, D_MODEL)


def kernel(x, norm_w, w_in, s5_a_re, s5_a_im, s5_log_step, s5_b_re, s5_b_im, s5_c_re, s5_c_im,
           s5_d, s5_glu_w, s5_glu_b, q_norm_w, k_norm_w, conv_w, conv_b, dt_bias, ssd_a_log,
           ssd_d, ssd_norm_w, proj_a, proj_b, proj_c, w_out):
    params = (norm_w, w_in, s5_a_re, s5_a_im, s5_log_step, s5_b_re, s5_b_im, s5_c_re, s5_c_im,
              s5_d, s5_glu_w, s5_glu_b, q_norm_w, k_norm_w, conv_w, conv_b, dt_bias, ssd_a_log,
              ssd_d, ssd_norm_w, proj_a, proj_b, proj_c, w_out)
    for i in range(norm_w.shape[0]):
        x = _layer(x, *(p[i] for p in params))
    return x
```

```python
import functools
import math

import numpy as np
import jax
import jax.numpy as jnp
from jax import lax
from jax.experimental import pallas as pl
from jax.experimental.pallas import tpu as pltpu

D_MODEL = 1024
RMS_EPS = 1e-6
LANES = 128

S5_WIDTH = 512
S5_GROUP = 16
S5_GROUPS = 32
S5_STATE = 64
S5_CHUNK = 16
S5_PAIRS = S5_GROUPS // 2
S5_TILES = S5_WIDTH // LANES
S5_PPT = S5_PAIRS // S5_TILES
S5_PW = 2 * S5_GROUP
S5_TT = 512

ATT_HEAD_DIM = 64
ATT_PAIRS = ((128, 1), (512, 4), (2048, 16))
ATT_HPG = 4
ATT_WIDTH = 768
ATT_GW = ATT_HPG * ATT_HEAD_DIM
ATT_BLOCK = 128
ATT_TILES = ATT_WIDTH // LANES
ATT_OUT_TILES = ATT_GW // LANES
ATT_TT = 2048

SSD_HEAD_DIM = 64
SSD_WIDTH = 768
SSD_HEADS = 12
SSD_GROUPS = 2
SSD_STATE = 128
SSD_CONV = 4
SSD_CHUNK = 128
SSD_CONV_DIM = 1280
SSD_DT_PAD = 128
SSD_IN = SSD_CONV_DIM + SSD_DT_PAD
SSD_TT = 512

IN_SPLITS = (512, 512, 768, 768, 768, 256, 1280, 12, 768, 3072)
_OFF = np.concatenate([[0], np.cumsum(IN_SPLITS)]).tolist()
(O_UA, O_ZA, O_Q, O_K, O_V, O_ZB, O_XBC, O_DT, O_ZC, O_GATE, O_END) = _OFF

A_UA, A_Q, A_K, A_V, A_XBC = 0, 512, 1280, 2048, 2816
A_WIDTH = A_XBC + SSD_IN
Z_ZA, Z_ZB, Z_ZC, Z_GATE = 0, 512, 768, 1536
Z_WIDTH = Z_GATE + 3 * D_MODEL

ROW_TILE = 512
VMEM_LIMIT = 56 * 1024 * 1024

BF16 = jnp.bfloat16
F32 = jnp.float32


def _dot(a, b):
    return jnp.dot(a, b, preferred_element_type=F32)


def _dot_nt(a, b):
    return lax.dot_general(a, b, (((1,), (1,)), ((), ())), preferred_element_type=F32)


def _const_spec(shape):
    nd = len(shape)
    return pl.BlockSpec(shape, lambda *_: (0,) * nd, pipeline_mode=pl.Buffered(1))


def _sigmoid(x):
    return 1.0 / (1.0 + jnp.exp(-x))


def _silu(x):
    return x * _sigmoid(x)


def _rms_rows(x, w):
    return x * lax.rsqrt(jnp.mean(x * x, axis=-1, keepdims=True) + RMS_EPS) * w


def _inproj_kernel(x_ref, nw_ref, w_ref, qw_ref, kw_ref, ones_ref,
                   ua_ref, q_ref, k_ref, v_ref, xd_ref):
    h = _rms_rows(x_ref[...], nw_ref[...]).astype(BF16)
    for c in range(0, S5_TILES, 2):
        t = _dot(h, w_ref[:, A_UA + c * LANES:A_UA + (c + 2) * LANES])
        ua_ref[c] = t[:, :LANES]
        ua_ref[c + 1] = t[:, LANES:]
    for col, ref in ((A_Q, q_ref), (A_K, k_ref), (A_V, v_ref)):
        for j in range(ATT_WIDTH // ATT_GW):
            t = _dot(h, w_ref[:, col + j * ATT_GW:col + (j + 1) * ATT_GW])
            ref[2 * j] = t[:, :LANES]
            ref[2 * j + 1] = t[:, LANES:]
    for lo in range(0, SSD_IN, 2 * LANES):
        hi = min(lo + 2 * LANES, SSD_IN)
        xd_ref[:, lo:hi] = _dot(h, w_ref[:, A_XBC + lo:A_XBC + hi])

    ones = ones_ref[...]
    for ref, gain_ref in ((q_ref, qw_ref), (k_ref, kw_ref)):
        for j in range(ATT_WIDTH // ATT_GW):
            t = jnp.concatenate([ref[2 * j], ref[2 * j + 1]], axis=1)
            ms = _dot((t * t).astype(BF16), ones)
            t = t * lax.rsqrt(ms + RMS_EPS) * gain_ref[:, j * ATT_GW:(j + 1) * ATT_GW]
            ref[2 * j] = t[:, :LANES]
            ref[2 * j + 1] = t[:, LANES:]


def _time_major_block(i, batch, seq, tm, tt):
    per_seq = seq // tm
    b, r = i // per_seq, i % per_seq
    sub = tt // tm
    return ((r // sub) * batch + b) * sub + r % sub


def _inproj(x2, norm_w, w_a, qw, kw, ones, batch, seq):
    n = x2.shape[0]
    tm = ROW_TILE
    row = lambda w: pl.BlockSpec((tm, w), lambda i: (i, 0))
    tiled = lambda t: pl.BlockSpec((t, tm, LANES), lambda i: (0, i, 0))
    s5_tiled = pl.BlockSpec(
        (S5_TILES, tm, LANES),
        lambda i: (0, _time_major_block(i, batch, seq, tm, min(S5_TT, seq)), 0))
    return pl.pallas_call(
        _inproj_kernel,
        grid=(n // tm,),
        in_specs=[row(D_MODEL), _const_spec((1, D_MODEL)), _const_spec((D_MODEL, A_WIDTH)),
                  _const_spec((1, ATT_WIDTH)), _const_spec((1, ATT_WIDTH)),
                  _const_spec((ATT_GW, ATT_GW))],
        out_specs=[s5_tiled, tiled(ATT_TILES), tiled(ATT_TILES), tiled(ATT_TILES), row(SSD_IN)],
        out_shape=[jax.ShapeDtypeStruct((S5_TILES, n, LANES), F32),
                   jax.ShapeDtypeStruct((ATT_TILES, n, LANES), F32),
                   jax.ShapeDtypeStruct((ATT_TILES, n, LANES), F32),
                   jax.ShapeDtypeStruct((ATT_TILES, n, LANES), F32),
                   jax.ShapeDtypeStruct((n, SSD_IN), F32)],
        compiler_params=pltpu.CompilerParams(dimension_semantics=("parallel",),
                                             vmem_limit_bytes=VMEM_LIMIT),
        name="inproj",
    )(x2, norm_w, w_a, qw, kw, ones)


def _s5_weights(a_re, a_im, log_step, b_re, b_im, c_re, c_im, d):
    G, P, I, T = S5_GROUPS, S5_STATE, S5_GROUP, S5_CHUNK
    a_re, a_im = a_re.astype(F32), a_im.astype(F32)
    b_re, b_im = b_re.astype(F32), b_im.astype(F32)
    c_re, c_im = c_re.astype(F32), c_im.astype(F32)
    step = jnp.exp(log_step.astype(F32))[:, None]
    mag = jnp.exp(a_re * step)
    ang = a_im * step
    lam_re, lam_im = mag * jnp.cos(ang), mag * jnp.sin(ang)
    num_re, num_im = lam_re - 1.0, lam_im
    den = a_re * a_re + a_im * a_im
    f_re = (num_re * a_re + num_im * a_im) / den
    f_im = (num_im * a_re - num_re * a_im) / den
    bb_re = f_re[..., None] * b_re - f_im[..., None] * b_im
    bb_im = f_re[..., None] * b_im + f_im[..., None] * b_re
    pw_re, pw_im = [jnp.ones_like(lam_re)], [jnp.zeros_like(lam_im)]
    for _ in range(T):
        r, i = pw_re[-1], pw_im[-1]
        pw_re.append(r * lam_re - i * lam_im)
        pw_im.append(r * lam_im + i * lam_re)
    pw_re, pw_im = jnp.stack(pw_re), jnp.stack(pw_im)
    hp = lax.Precision.HIGHEST
    rows = T * S5_PW
    pw_re = pw_re.reshape(T + 1, S5_PAIRS, 1, 2 * P).transpose(1, 0, 2, 3)
    pw_im = pw_im.reshape(T + 1, S5_PAIRS, 1, 2 * P).transpose(1, 0, 2, 3)

    def blockdiag(w):
        w = w.reshape((S5_PAIRS, 2) + w.shape[1:])
        z = jnp.zeros_like(w[:, 0])
        return jnp.concatenate([jnp.concatenate([w[:, 0], z], axis=-1),
                                jnp.concatenate([z, w[:, 1]], axis=-1)], axis=1)[:, None]

    bt_re, bt_im = blockdiag(bb_re.transpose(0, 2, 1)), blockdiag(bb_im.transpose(0, 2, 1))
    ct_re, ct_im = blockdiag(c_re), blockdiag(c_im)

    def scaled(w_re, w_im, q_re, q_im):
        re = (w_re * q_re - w_im * q_im).reshape(S5_PAIRS, -1, 2 * P)
        im = (w_re * q_im + w_im * q_re).reshape(S5_PAIRS, -1, 2 * P)
        return re, im

    wst_re, wst_im = scaled(bt_re, bt_im, pw_re[:, T - 1::-1][:, :T], pw_im[:, T - 1::-1][:, :T])
    wst = jnp.concatenate([wst_re, wst_im], axis=2)
    wo_re, wo_im = scaled(ct_re, ct_im, pw_re[:, 1:T + 1], pw_im[:, 1:T + 1])
    wo = jnp.concatenate([wo_re, -wo_im], axis=2)
    g_re, g_im = scaled(ct_re, ct_im, pw_re[:, 0:T], pw_im[:, 0:T])
    brow = (jnp.einsum('rap,rcp->rac', bt_re[:, 0], g_re, precision=hp)
            - jnp.einsum('rap,rcp->rac', bt_im[:, 0], g_im, precision=hp))
    skip = jnp.eye(S5_PW, dtype=F32)[None] * d.astype(F32).reshape(S5_PAIRS, 1, S5_PW)
    brow = brow + jnp.pad(skip, ((0, 0), (0, 0), (0, rows - S5_PW)))
    brow = brow.astype(BF16)
    wtoep = jnp.stack(
        [jnp.pad(brow, ((0, 0), (0, 0), (s * S5_PW, 0)))[:, :, :rows] for s in range(T)], axis=0)
    lam = jnp.stack([pw_re[:, T, 0], pw_im[:, T, 0]], axis=1)
    return wst.astype(BF16), wtoep, wo.astype(BF16), lam


def _s5_perm():
    n = S5_PPT * LANES
    a = np.arange(n)
    k, pp, j = a // LANES, (a % LANES) // S5_PW, a % S5_PW
    p = np.zeros((n, n), np.float32)
    p[a, pp * LANES + k * S5_PW + j] = 1.0
    return p


def _s5_kernel(u_ref, wst_ref, wt_ref, wo_ref, lam_ref, perm_ref, permt_ref, y_ref,
               x_s, uc_s, s_s, h_s, cre_s, cim_s, *, batch, tt):
    nch = tt // S5_CHUNK
    quads = S5_CHUNK * S5_PW // LANES
    per_q = LANES // S5_PW

    @pl.when(pl.program_id(1) == 0)
    def _():
        cre_s[...] = jnp.zeros_like(cre_s)
        cim_s[...] = jnp.zeros_like(cim_s)

    u2 = u_ref.at[0]
    y2 = y_ref.at[0]

    def gather(n, carry):
        r0 = pl.multiple_of(n * batch, batch)
        for q in range(quads):
            for k in range(per_q):
                x_s[q, pl.ds(r0, batch), k * LANES:(k + 1) * LANES] = (
                    u2[pl.ds(n * S5_CHUNK + q * per_q + k, batch, stride=tt), :])
        return carry

    lax.fori_loop(0, nch, gather, 0)

    perm = perm_ref[...]
    for q in range(quads):
        ucq = _dot(x_s[q].astype(BF16), perm)
        for pp in range(S5_PPT):
            uc_s[pp, :, q * LANES:(q + 1) * LANES] = ucq[:, pp * LANES:(pp + 1) * LANES].astype(BF16)

    for pp in range(S5_PPT):
        u = uc_s[pp]
        s_s[...] = _dot(u, wst_ref[pp])
        lre = jnp.broadcast_to(lam_ref[pp, 0:1, :], (batch, LANES))
        lim = jnp.broadcast_to(lam_ref[pp, 1:2, :], (batch, LANES))

        def scan(n, carry):
            hr, hi = carry
            r0 = pl.multiple_of(n * batch, batch)
            h_s[pl.ds(r0, batch), 0:LANES] = hr
            h_s[pl.ds(r0, batch), LANES:2 * LANES] = hi
            sr = s_s[pl.ds(r0, batch), 0:LANES]
            si = s_s[pl.ds(r0, batch), LANES:2 * LANES]
            return (lre * hr - lim * hi + sr, lre * hi + lim * hr + si)

        hr, hi = lax.fori_loop(0, nch, scan, (cre_s[pp], cim_s[pp]))
        cre_s[pp] = hr
        cim_s[pp] = hi
        wt = wt_ref[:, pp].reshape(S5_CHUNK * S5_PW, S5_CHUNK * S5_PW)
        y = _dot(u, wt) + _dot_nt(h_s[...].astype(BF16), wo_ref[pp])
        for q in range(quads):
            x_s[q, :, pp * LANES:(pp + 1) * LANES] = y[:, q * LANES:(q + 1) * LANES]

    permt = permt_ref[...]
    for q in range(quads):
        yq = x_s[q]
        hi = yq.astype(BF16)
        lo = (yq - hi.astype(F32)).astype(BF16)
        x_s[q] = _dot(hi, permt) + _dot(lo, permt)

    def scatter(n, carry):
        r0 = pl.multiple_of(n * batch, batch)
        for q in range(quads):
            for k in range(per_q):
                y2[pl.ds(n * S5_CHUNK + q * per_q + k, batch, stride=tt), :] = (
                    x_s[q, pl.ds(r0, batch), k * LANES:(k + 1) * LANES])
        return carry

    lax.fori_loop(0, nch, scatter, 0)


def _s5_scan(u4, weights, b, s):
    wst, wt, wo, lam = weights
    tt = min(S5_TT, s)
    rows = tt // S5_CHUNK * b
    width = S5_CHUNK * S5_PW
    perm = _s5_perm()
    act = pl.BlockSpec((1, b * tt, LANES), lambda c, j: (c, j, 0))
    wspec = lambda a_, b_: pl.BlockSpec((S5_PPT, a_, b_), lambda c, j: (c, 0, 0))
    return pl.pallas_call(
        functools.partial(_s5_kernel, batch=b, tt=tt),
        grid=(S5_TILES, s // tt),
        in_specs=[act, wspec(width, 2 * LANES),
                  pl.BlockSpec((S5_CHUNK, S5_PPT, S5_PW, width), lambda c, j: (0, c, 0, 0)),
                  wspec(width, 2 * LANES),
                  wspec(2, LANES), _const_spec((width, width)), _const_spec((width, width))],
        out_specs=act,
        out_shape=jax.ShapeDtypeStruct(u4.shape, F32),
        scratch_shapes=[pltpu.VMEM((S5_PPT, rows, width), F32),
                        pltpu.VMEM((S5_PPT, rows, width), BF16),
                        pltpu.VMEM((rows, 2 * LANES), F32), pltpu.VMEM((rows, 2 * LANES), F32),
                        pltpu.VMEM((S5_PPT, b, LANES), F32), pltpu.VMEM((S5_PPT, b, LANES), F32)],
        compiler_params=pltpu.CompilerParams(dimension_semantics=("parallel", "arbitrary"),
                                             vmem_limit_bytes=VMEM_LIMIT),
        name="s5_scan",
    )(u4, wst, wt, wo, lam, jnp.asarray(perm, BF16), jnp.asarray(perm.T, BF16))


def _attn_kernel(q_ref, k_ref, v_ref, ones_ref, o_ref,
                 kring, vring, sc0, sp0, sc1, sp1, m_s, l_s, n_s, *, tt):
    tile = pl.program_id(1)
    step = pl.program_id(2)
    ng = len(ATT_PAIRS)
    group = step % ng
    src = group * ATT_OUT_TILES + step // ng
    qv, kv, vv, ov = q_ref.at[0, 0], k_ref.at[0, 0], v_ref.at[0, 0], o_ref.at[0, 0]
    nblk = tt // ATT_BLOCK
    ring = 2 * nblk
    half = (tile % 2) * nblk

    @pl.when(tile == 0)
    def _():
        zero = jnp.zeros((nblk, ATT_BLOCK, LANES), BF16)
        kring[src, nblk:ring] = zero
        vring[src, nblk:ring] = zero

    lane_lo = lax.broadcasted_iota(jnp.int32, (1, LANES), 1) < ATT_HEAD_DIM
    row = lax.broadcasted_iota(jnp.int32, (2 * ATT_BLOCK, ATT_BLOCK), 0) % ATT_BLOCK
    col = lax.broadcasted_iota(jnp.int32, (2 * ATT_BLOCK, ATT_BLOCK), 1)
    mask_cur = col <= row
    mask_prev = col >= row
    neg = jnp.float32(-jnp.inf)
    ones = ones_ref[...]

    def halves(t):
        return jnp.where(lane_lo, t[:ATT_BLOCK], t[ATT_BLOCK:])

    def rows_of(i, dil):
        return pl.ds((i // dil) * (ATT_BLOCK * dil) + i % dil, ATT_BLOCK, stride=dil)

    def stage_a(i, dil, sc_buf, sp_buf):
        rows = rows_of(i, dil)
        q = qv[rows, :]
        kc = kv[rows, :].astype(BF16)
        slot = half + i
        kring[src, slot] = kc
        vring[src, slot] = vv[rows, :].astype(BF16)
        kp = kring[src, (slot + ring - dil) % ring]
        has_prev = jnp.logical_or(tile > 0, i >= dil)
        q2 = jnp.concatenate([jnp.where(lane_lo, q, 0.0), jnp.where(lane_lo, 0.0, q)],
                             axis=0).astype(BF16)
        sc_buf[...] = jnp.where(mask_cur, _dot_nt(q2, kc), neg)
        sp_buf[...] = jnp.where(jnp.logical_and(mask_prev, has_prev), _dot_nt(q2, kp), neg)

    def stage_b(i, dil, gi, sc_buf, sp_buf):
        rows = rows_of(i, dil)
        slot = half + i
        sc, sp = sc_buf[...], sp_buf[...]
        m = jnp.max(jnp.maximum(sc, sp), axis=-1, keepdims=True)
        pc = jnp.exp2(sc - m).astype(BF16)
        pp = jnp.exp2(sp - m).astype(BF16)
        vc = jnp.concatenate([vring[src, slot], ones], axis=1)
        vp = jnp.concatenate([vring[src, (slot + ring - dil) % ring], ones], axis=1)
        r = _dot(pc, vc) + _dot(pp, vp)
        acc = halves(r[:, :LANES])
        lb = halves(r[:, LANES:])
        mb = halves(jnp.broadcast_to(m, (2 * ATT_BLOCK, LANES)))
        if gi > 0:
            m_old, l_old, n_old = m_s[rows, :], l_s[rows, :], n_s[rows, :]
            m_new = jnp.maximum(m_old, mb)
            a_old = jnp.exp2(m_old - m_new)
            a_new = jnp.exp2(mb - m_new)
            acc = n_old * a_old + acc * a_new
            lb = l_old * a_old + lb * a_new
            mb = m_new
        if gi == ng - 1:
            ov[rows, :] = acc / lb
        else:
            m_s[rows, :] = mb
            l_s[rows, :] = lb
            n_s[rows, :] = acc

    for gi, (_, dil) in enumerate(ATT_PAIRS):
        @pl.when(group == gi)
        def _(gi=gi, dil=dil):
            stage_a(0, dil, sc0, sp0)
            stage_a(1, dil, sc1, sp1)

            def body(ii, c):
                i0 = 2 * ii
                stage_b(i0, dil, gi, sc0, sp0)
                stage_b(i0 + 1, dil, gi, sc1, sp1)
                stage_a(jnp.minimum(i0 + 2, nblk - 1), dil, sc0, sp0)
                stage_a(jnp.minimum(i0 + 3, nblk - 1), dil, sc1, sp1)
                return c

            lax.fori_loop(0, nblk // 2, body, 0)


def _attention(q6, k6, v6):
    _, b, s, _ = q6.shape
    tt = min(ATT_TT, s)
    nblk = tt // ATT_BLOCK
    ng = len(ATT_PAIRS)
    src = lambda i: (i % ng) * ATT_OUT_TILES + i // ng
    blk = (1, 1, tt, LANES)
    cur = pl.BlockSpec(blk, lambda bi, t, i: (src(i), bi, t, 0))
    out = pl.BlockSpec(blk, lambda bi, t, i: (i // ng, bi, t, 0))
    ones = jnp.ones((ATT_BLOCK, LANES), BF16)
    score = pltpu.VMEM((2 * ATT_BLOCK, ATT_BLOCK), F32)
    return pl.pallas_call(
        functools.partial(_attn_kernel, tt=tt),
        grid=(b, s // tt, ATT_TILES),
        in_specs=[cur, cur, cur, _const_spec((ATT_BLOCK, LANES))],
        out_specs=out,
        out_shape=jax.ShapeDtypeStruct((ATT_OUT_TILES, b, s, LANES), F32),
        scratch_shapes=[pltpu.VMEM((ATT_TILES, 2 * nblk, ATT_BLOCK, LANES), BF16)] * 2
        + [score] * 4 + [pltpu.VMEM((tt, LANES), F32)] * 3,
        compiler_params=pltpu.CompilerParams(
            dimension_semantics=("parallel", "arbitrary", "arbitrary"),
            vmem_limit_bytes=VMEM_LIMIT),
        name="attention",
    )(q6, k6, v6, ones)


def _ssd_kernel(xd_ref, cw_ref, cb_ref, dtb_ref, alog_ref, dsk_ref, tril_ref, y_ref,
                xpad_s, state_s):
    @pl.when(pl.program_id(1) == 0)
    def _():
        xpad_s[0:8, :] = jnp.zeros((8, SSD_CONV_DIM), F32)
        state_s[...] = jnp.zeros_like(state_s)

    def chunk(ci, carry):
        rows = pl.ds(pl.multiple_of(ci * SSD_CHUNK, SSD_CHUNK), SSD_CHUNK)
        _ssd_chunk(rows, xd_ref, cw_ref, cb_ref, dtb_ref, alog_ref, dsk_ref, tril_ref, y_ref,
                   xpad_s, state_s)
        return carry

    lax.fori_loop(0, xd_ref.shape[1] // SSD_CHUNK, chunk, 0)


def _ssd_chunk(rows, xd_ref, cw_ref, cb_ref, dtb_ref, alog_ref, dsk_ref, tril_ref, y_ref,
               xpad_s, state_s):
    L = SSD_CHUNK
    xpad_s[8:8 + L, :] = xd_ref[0, rows, 0:SSD_CONV_DIM]
    xe = xpad_s[...]
    acc = cb_ref[...] + cw_ref[SSD_CONV - 1:SSD_CONV, :] * xe[8:8 + L]
    for back in range(1, SSD_CONV):
        kk = SSD_CONV - 1 - back
        acc = acc + cw_ref[kk:kk + 1, :] * pltpu.roll(xe, back, axis=0)[8:8 + L]
    xpad_s[0:8, :] = xe[L:L + 8]
    xc = _silu(acc)

    lane = lax.broadcasted_iota(jnp.int32, (1, LANES), 1)
    dtr = xd_ref[0, rows, SSD_CONV_DIM:SSD_IN] + dtb_ref[...]
    dt = jnp.maximum(dtr, 0.0) + jnp.log(1.0 + jnp.exp(-jnp.abs(dtr)))
    a = jnp.where(lane < SSD_HEADS, -jnp.exp(alog_ref[...]) * math.log2(math.e), 0.0)
    a_dt = dt * a
    a_cs = jnp.dot(tril_ref[...], a_dt, preferred_element_type=F32,
                   precision=lax.Precision.HIGHEST)
    ea = jnp.exp2(a_cs)
    nhp = 16
    last_t = a_cs.T[0:nhp, L - 1:L]
    r_t = (a_cs - jnp.log2(dt)).T[0:nhp]
    w_t = jnp.exp2(last_t - r_t)
    el_t = jnp.exp2(last_t)

    row = lax.broadcasted_iota(jnp.int32, (L, L), 0)
    col = lax.broadcasted_iota(jnp.int32, (L, L), 1)
    tri = row >= col
    lane_lo = lane < SSD_HEAD_DIM
    neg = jnp.float32(-jnp.inf)

    for g in range(SSD_GROUPS):
        bm = xc[:, SSD_WIDTH + g * SSD_STATE:SSD_WIDTH + (g + 1) * SSD_STATE]
        cm = xc[:, SSD_WIDTH + (SSD_GROUPS + g) * SSD_STATE:
                SSD_WIDTH + (SSD_GROUPS + g + 1) * SSD_STATE].astype(BF16)
        cb = _dot_nt(cm, bm.astype(BF16))
        bm_t = bm.T
        for pr in range(3 * g, 3 * g + 3):
            xs = xc[:, pr * LANES:(pr + 1) * LANES]
            x_lo = jnp.where(lane_lo, xs, 0.0).astype(BF16)
            x_hi = jnp.where(lane_lo, 0.0, xs).astype(BF16)
            mats, bws, cols, decs = [], [], [], []
            for h in (2 * pr, 2 * pr + 1):
                c_col = a_cs[:, h:h + 1]
                dec = jnp.exp2(jnp.where(tri, c_col - r_t[h:h + 1, :], neg))
                mats.append((cb * dec).astype(BF16))
                bws.append((bm_t * w_t[h:h + 1, :]).astype(BF16))
                cols.append(ea[:, h:h + 1])
                decs.append(el_t[h:h + 1, :])
            st = state_s[pr]
            y_diag = _dot(mats[0], x_lo) + _dot(mats[1], x_hi)
            y_off = _dot(cm, st.astype(BF16)) * jnp.where(lane_lo, cols[0], cols[1])
            y_ref[0, rows, pr * LANES:(pr + 1) * LANES] = (
                y_diag + y_off + xs * dsk_ref[:, pr * LANES:(pr + 1) * LANES])
            inc = _dot(bws[0], x_lo) + _dot(bws[1], x_hi)
            state_s[pr] = st * jnp.where(lane_lo, decs[0], decs[1]) + inc


def _ssd_scan(xd, conv_w, conv_b, dt_bias, a_log, d):
    b, s, _ = xd.shape
    tt = min(SSD_TT, s)
    pad = lambda t: jnp.pad(t.astype(F32), (0, SSD_DT_PAD - SSD_HEADS)).reshape(1, SSD_DT_PAD)
    dsk = jnp.repeat(d.astype(F32), SSD_HEAD_DIM).reshape(1, SSD_WIDTH)
    tril = jnp.asarray(np.tril(np.ones((SSD_CHUNK, SSD_CHUNK), np.float32)))
    return pl.pallas_call(
        _ssd_kernel,
        grid=(b, s // tt),
        in_specs=[pl.BlockSpec((1, tt, SSD_IN), lambda bi, c: (bi, c, 0)),
                  _const_spec((SSD_CONV, SSD_CONV_DIM)), _const_spec((1, SSD_CONV_DIM)),
                  _const_spec((1, SSD_DT_PAD)), _const_spec((1, SSD_DT_PAD)),
                  _const_spec((1, SSD_WIDTH)), _const_spec((SSD_CHUNK, SSD_CHUNK))],
        out_specs=pl.BlockSpec((1, tt, SSD_WIDTH), lambda bi, c: (bi, c, 0)),
        out_shape=jax.ShapeDtypeStruct((b, s, SSD_WIDTH), F32),
        scratch_shapes=[pltpu.VMEM((SSD_CHUNK + 8, SSD_CONV_DIM), F32),
                        pltpu.VMEM((SSD_HEADS // 2, SSD_STATE, LANES), F32)],
        compiler_params=pltpu.CompilerParams(dimension_semantics=("parallel", "arbitrary"),
                                             vmem_limit_bytes=VMEM_LIMIT),
        name="ssd_scan",
    )(xd, conv_w.astype(F32), conv_b.astype(F32).reshape(1, SSD_CONV_DIM),
      pad(dt_bias), pad(a_log), dsk, tril)


def _gelu_tanh(x):
    c = math.sqrt(2.0 / math.pi)
    return 0.5 * x * (1.0 + jnp.tanh(c * (x + 0.044715 * (x * x * x))))


def _merge_kernel(x_ref, nw_ref, wz_ref, ya_ref, yb_ref, yc_ref, gw_ref, gb_ref, snw_ref,
                  pa_ref, pb_ref, pc_ref, wo_ref, out_ref):
    x = x_ref[...]
    h = _rms_rows(x, nw_ref[...]).astype(BF16)

    def gate(i):
        lo = Z_GATE + i * D_MODEL
        return _sigmoid(_dot(h, wz_ref[:, lo:lo + D_MODEL]))

    g = _gelu_tanh(jnp.concatenate([ya_ref[c] for c in range(S5_TILES)], axis=1))
    y_a = g * _sigmoid(_dot(g.astype(BF16), gw_ref[...]) + gb_ref[...])
    y_a = y_a * _silu(_dot(h, wz_ref[:, Z_ZA:Z_ZB]))
    merged = gate(0) * _dot(y_a.astype(BF16), pa_ref[...])

    y_b = jnp.concatenate([yb_ref[c] for c in range(ATT_OUT_TILES)], axis=1)
    y_b = y_b * _silu(_dot(h, wz_ref[:, Z_ZB:Z_ZC]))
    merged = merged + gate(1) * _dot(y_b.astype(BF16), pb_ref[...])

    y_c = yc_ref[...] * _silu(_dot(h, wz_ref[:, Z_ZC:Z_GATE]))
    y_c = _rms_rows(y_c, snw_ref[...])
    merged = merged + gate(2) * _dot(y_c.astype(BF16), pc_ref[...])

    out_ref[...] = x + _dot(merged.astype(BF16), wo_ref[...])


def _merge(x2, norm_w, w_z, ya4, yb2, yc, glu_w, glu_b, ssd_nw, pa, pb, pc, wo, batch, seq):
    n = x2.shape[0]
    tm = ROW_TILE
    row = lambda w: pl.BlockSpec((tm, w), lambda i: (i, 0))
    tiled = lambda t: pl.BlockSpec((t, tm, LANES), lambda i: (0, i, 0))
    s5_tiled = pl.BlockSpec(
        (S5_TILES, tm, LANES),
        lambda i: (0, _time_major_block(i, batch, seq, tm, min(S5_TT, seq)), 0))
    return pl.pallas_call(
        _merge_kernel,
        grid=(n // tm,),
        in_specs=[row(D_MODEL), _const_spec((1, D_MODEL)), _const_spec((D_MODEL, Z_WIDTH)),
                  s5_tiled, tiled(ATT_OUT_TILES), row(SSD_WIDTH),
                  _const_spec((S5_WIDTH, S5_WIDTH)), _const_spec((1, S5_WIDTH)),
                  _const_spec((1, SSD_WIDTH)), _const_spec((S5_WIDTH, D_MODEL)),
                  _const_spec((ATT_GW, D_MODEL)), _const_spec((SSD_WIDTH, D_MODEL)),
                  _const_spec((D_MODEL, D_MODEL))],
        out_specs=row(D_MODEL),
        out_shape=jax.ShapeDtypeStruct((n, D_MODEL), F32),
        compiler_params=pltpu.CompilerParams(dimension_semantics=("parallel",),
                                             vmem_limit_bytes=VMEM_LIMIT),
        name="merge",
    )(x2, norm_w, w_z, ya4, yb2, yc, glu_w, glu_b, ssd_nw, pa, pb, pc, wo)


def _layer(x, norm_w, w_in, s5_a_re, s5_a_im, s5_log_step, s5_b_re, s5_b_im, s5_c_re,
           s5_c_im, s5_d, s5_glu_w, s5_glu_b, q_norm_w, k_norm_w, conv_w, conv_b,
           dt_bias, ssd_a_log, ssd_d, ssd_norm_w, proj_a, proj_b, proj_c, w_out):
    b, s, _ = x.shape
    n = b * s
    x2 = x.reshape(n, D_MODEL)
    nw = norm_w.astype(F32).reshape(1, D_MODEL)
    w_a = jnp.concatenate(
        [w_in[:, O_UA:O_ZA], w_in[:, O_Q:O_ZB], w_in[:, O_XBC:O_ZC],
         jnp.zeros((D_MODEL, SSD_DT_PAD - SSD_HEADS), BF16)], axis=1)
    w_z = jnp.concatenate(
        [w_in[:, O_ZA:O_Q], w_in[:, O_ZB:O_XBC], w_in[:, O_ZC:O_END]], axis=1)
    qw = jnp.tile(q_norm_w.astype(F32) * (ATT_HEAD_DIM ** -0.5 * math.log2(math.e)),
                  ATT_WIDTH // ATT_HEAD_DIM)
    kw = jnp.tile(k_norm_w.astype(F32), ATT_WIDTH // ATT_HEAD_DIM)
    blk = np.arange(ATT_GW) // ATT_HEAD_DIM
    ones = jnp.asarray((blk[:, None] == blk[None, :]).astype(np.float32) / ATT_HEAD_DIM, BF16)

    u4, q6, k6, v6, xd = _inproj(x2, nw, w_a, qw.reshape(1, -1), kw.reshape(1, -1), ones, b, s)

    s5w = _s5_weights(s5_a_re, s5_a_im, s5_log_step, s5_b_re, s5_b_im, s5_c_re, s5_c_im, s5_d)
    ya4 = _s5_scan(u4, s5w, b, s)

    tile4 = lambda t: t.reshape(ATT_TILES, b, s, LANES)
    yb2 = _attention(tile4(q6), tile4(k6), tile4(v6)).reshape(ATT_OUT_TILES, n, LANES)

    y_c = _ssd_scan(xd.reshape(b, s, SSD_IN), conv_w, conv_b, dt_bias, ssd_a_log, ssd_d)

    out = _merge(x2, nw, w_z, ya4, yb2, y_c.reshape(n, SSD_WIDTH),
                 s5_glu_w.astype(BF16), s5_glu_b.astype(F32).reshape(1, S5_WIDTH),
                 ssd_norm_w.astype(F32).reshape(1, SSD_WIDTH),
                 proj_a.astype(BF16), proj_b.astype(BF16), proj_c.astype(BF16),
                 w_out.astype(BF16), b, s)
    return out.reshape(b, s, D_MODEL)


def kernel(x, norm_w, w_in, s5_a_re, s5_a_im, s5_log_step, s5_b_re, s5_b_im, s5_c_re, s5_c_im,
           s5_d, s5_glu_w, s5_glu_b, q_norm_w, k_norm_w, conv_w, conv_b, dt_bias, ssd_a_log,
           ssd_d, ssd_norm_w, proj_a, proj_b, proj_c, w_out):
    w_in = w_in.astype(BF16)
    params = (norm_w, w_in, s5_a_re, s5_a_im, s5_log_step, s5_b_re, s5_b_im, s5_c_re, s5_c_im,
              s5_d, s5_glu_w, s5_glu_b, q_norm_w, k_norm_w, conv_w, conv_b, dt_bias, ssd_a_log,
              ssd_d, ssd_norm_w, proj_a, proj_b, proj_c, w_out)
    for i in range(norm_w.shape[0]):
        x = _layer(x, *(p[i] for p in params))
    return x
```

```python
import functools
import math

import numpy as np
import jax
import jax.numpy as jnp
from jax import lax
from jax.experimental import pallas as pl
from jax.experimental.pallas import tpu as pltpu

D_MODEL = 1024
RMS_EPS = 1e-6
LANES = 128

S5_WIDTH = 512
S5_GROUP = 16
S5_GROUPS = 32
S5_STATE = 64
S5_CHUNK = 16
S5_PAIRS = S5_GROUPS // 2
S5_TILES = S5_WIDTH // LANES
S5_PPT = S5_PAIRS // S5_TILES
S5_PW = 2 * S5_GROUP
S5_TT = 512

ATT_HEAD_DIM = 64
ATT_PAIRS = ((128, 1), (512, 4), (2048, 16))
ATT_HPG = 4
ATT_WIDTH = 768
ATT_GW = ATT_HPG * ATT_HEAD_DIM
ATT_BLOCK = 128
ATT_TILES = ATT_WIDTH // LANES
ATT_OUT_TILES = ATT_GW // LANES
ATT_TT = 2048

SSD_HEAD_DIM = 64
SSD_WIDTH = 768
SSD_HEADS = 12
SSD_GROUPS = 2
SSD_STATE = 128
SSD_CONV = 4
SSD_CHUNK = 128
SSD_CONV_DIM = 1280
SSD_DT_PAD = 128
SSD_IN = SSD_CONV_DIM + SSD_DT_PAD
SSD_TT = 512

IN_SPLITS = (512, 512, 768, 768, 768, 256, 1280, 12, 768, 3072)
_OFF = np.concatenate([[0], np.cumsum(IN_SPLITS)]).tolist()
(O_UA, O_ZA, O_Q, O_K, O_V, O_ZB, O_XBC, O_DT, O_ZC, O_GATE, O_END) = _OFF

A_UA, A_Q, A_K, A_V, A_XBC = 0, 512, 1280, 2048, 2816
A_WIDTH = A_XBC + SSD_IN
Z_ZA, Z_ZB, Z_ZC, Z_GATE = 0, 512, 768, 1536
Z_WIDTH = Z_GATE + 3 * D_MODEL

ROW_TILE = 512
VMEM_LIMIT = 56 * 1024 * 1024

BF16 = jnp.bfloat16
F32 = jnp.float32


def _dot(a, b):
    return jnp.dot(a, b, preferred_element_type=F32)


def _dot_nt(a, b):
    return lax.dot_general(a, b, (((1,), (1,)), ((), ())), preferred_element_type=F32)


def _const_spec(shape):
    nd = len(shape)
    return pl.BlockSpec(shape, lambda *_: (0,) * nd, pipeline_mode=pl.Buffered(1))


def _sigmoid(x):
    return 1.0 / (1.0 + jnp.exp(-x))


def _silu(x):
    return x * _sigmoid(x)


def _rms_rows(x, w):
    return x * lax.rsqrt(jnp.mean(x * x, axis=-1, keepdims=True) + RMS_EPS) * w


def _inproj_kernel(x_ref, nw_ref, w_ref, qw_ref, kw_ref, ones_ref,
                   ua_ref, q_ref, k_ref, v_ref, xd_ref):
    h = _rms_rows(x_ref[...], nw_ref[...]).astype(BF16)
    for c in range(0, S5_TILES, 2):
        t = _dot(h, w_ref[:, A_UA + c * LANES:A_UA + (c + 2) * LANES])
        ua_ref[c] = t[:, :LANES]
        ua_ref[c + 1] = t[:, LANES:]
    for col, ref in ((A_Q, q_ref), (A_K, k_ref), (A_V, v_ref)):
        for j in range(ATT_WIDTH // ATT_GW):
            t = _dot(h, w_ref[:, col + j * ATT_GW:col + (j + 1) * ATT_GW])
            ref[2 * j] = t[:, :LANES]
            ref[2 * j + 1] = t[:, LANES:]
    for lo in range(0, SSD_IN, 2 * LANES):
        hi = min(lo + 2 * LANES, SSD_IN)
        xd_ref[:, lo:hi] = _dot(h, w_ref[:, A_XBC + lo:A_XBC + hi])

    ones = ones_ref[...]
    for ref, gain_ref in ((q_ref, qw_ref), (k_ref, kw_ref)):
        for j in range(ATT_WIDTH // ATT_GW):
            t = jnp.concatenate([ref[2 * j], ref[2 * j + 1]], axis=1)
            ms = _dot((t * t).astype(BF16), ones)
            t = t * lax.rsqrt(ms + RMS_EPS) * gain_ref[:, j * ATT_GW:(j + 1) * ATT_GW]
            ref[2 * j] = t[:, :LANES]
            ref[2 * j + 1] = t[:, LANES:]


def _time_major_block(i, batch, seq, tm, tt):
    per_seq = seq // tm
    b, r = i // per_seq, i % per_seq
    sub = tt // tm
    return ((r // sub) * batch + b) * sub + r % sub


def _inproj(x2, norm_w, w_a, qw, kw, ones, batch, seq):
    n = x2.shape[0]
    tm = ROW_TILE
    row = lambda w: pl.BlockSpec((tm, w), lambda i: (i, 0))
    tiled = lambda t: pl.BlockSpec((t, tm, LANES), lambda i: (0, i, 0))
    s5_tiled = pl.BlockSpec(
        (S5_TILES, tm, LANES),
        lambda i: (0, _time_major_block(i, batch, seq, tm, min(S5_TT, seq)), 0))
    return pl.pallas_call(
        _inproj_kernel,
        grid=(n // tm,),
        in_specs=[row(D_MODEL), _const_spec((1, D_MODEL)), _const_spec((D_MODEL, A_WIDTH)),
                  _const_spec((1, ATT_WIDTH)), _const_spec((1, ATT_WIDTH)),
                  _const_spec((ATT_GW, ATT_GW))],
        out_specs=[s5_tiled, tiled(ATT_TILES), tiled(ATT_TILES), tiled(ATT_TILES), row(SSD_IN)],
        out_shape=[jax.ShapeDtypeStruct((S5_TILES, n, LANES), F32),
                   jax.ShapeDtypeStruct((ATT_TILES, n, LANES), F32),
                   jax.ShapeDtypeStruct((ATT_TILES, n, LANES), F32),
                   jax.ShapeDtypeStruct((ATT_TILES, n, LANES), F32),
                   jax.ShapeDtypeStruct((n, SSD_IN), F32)],
        compiler_params=pltpu.CompilerParams(dimension_semantics=("parallel",),
                                             vmem_limit_bytes=VMEM_LIMIT),
        name="inproj",
    )(x2, norm_w, w_a, qw, kw, ones)


def _s5_weights(a_re, a_im, log_step, b_re, b_im, c_re, c_im, d):
    G, P, I, T = S5_GROUPS, S5_STATE, S5_GROUP, S5_CHUNK
    a_re, a_im = a_re.astype(F32), a_im.astype(F32)
    b_re, b_im = b_re.astype(F32), b_im.astype(F32)
    c_re, c_im = c_re.astype(F32), c_im.astype(F32)
    step = jnp.exp(log_step.astype(F32))[:, None]
    mag = jnp.exp(a_re * step)
    ang = a_im * step
    lam_re, lam_im = mag * jnp.cos(ang), mag * jnp.sin(ang)
    num_re, num_im = lam_re - 1.0, lam_im
    den = a_re * a_re + a_im * a_im
    f_re = (num_re * a_re + num_im * a_im) / den
    f_im = (num_im * a_re - num_re * a_im) / den
    bb_re = f_re[..., None] * b_re - f_im[..., None] * b_im
    bb_im = f_re[..., None] * b_im + f_im[..., None] * b_re
    pw_re, pw_im = [jnp.ones_like(lam_re)], [jnp.zeros_like(lam_im)]
    for _ in range(T):
        r, i = pw_re[-1], pw_im[-1]
        pw_re.append(r * lam_re - i * lam_im)
        pw_im.append(r * lam_im + i * lam_re)
    pw_re, pw_im = jnp.stack(pw_re), jnp.stack(pw_im)
    hp = lax.Precision.HIGHEST
    rows = T * S5_PW
    pw_re = pw_re.reshape(T + 1, S5_PAIRS, 1, 2 * P).transpose(1, 0, 2, 3)
    pw_im = pw_im.reshape(T + 1, S5_PAIRS, 1, 2 * P).transpose(1, 0, 2, 3)

    def blockdiag(w):
        w = w.reshape((S5_PAIRS, 2) + w.shape[1:])
        z = jnp.zeros_like(w[:, 0])
        return jnp.concatenate([jnp.concatenate([w[:, 0], z], axis=-1),
                                jnp.concatenate([z, w[:, 1]], axis=-1)], axis=1)[:, None]

    bt_re, bt_im = blockdiag(bb_re.transpose(0, 2, 1)), blockdiag(bb_im.transpose(0, 2, 1))
    ct_re, ct_im = blockdiag(c_re), blockdiag(c_im)

    def scaled(w_re, w_im, q_re, q_im):
        re = (w_re * q_re - w_im * q_im).reshape(S5_PAIRS, -1, 2 * P)
        im = (w_re * q_im + w_im * q_re).reshape(S5_PAIRS, -1, 2 * P)
        return re, im

    wst_re, wst_im = scaled(bt_re, bt_im, pw_re[:, T - 1::-1][:, :T], pw_im[:, T - 1::-1][:, :T])
    wst = jnp.concatenate([wst_re, wst_im], axis=2)
    wo_re, wo_im = scaled(ct_re, ct_im, pw_re[:, 1:T + 1], pw_im[:, 1:T + 1])
    wo = jnp.concatenate([wo_re, -wo_im], axis=2)
    g_re, g_im = scaled(ct_re, ct_im, pw_re[:, 0:T], pw_im[:, 0:T])
    brow = (jnp.einsum('rap,rcp->rac', bt_re[:, 0], g_re, precision=hp)
            - jnp.einsum('rap,rcp->rac', bt_im[:, 0], g_im, precision=hp))
    skip = jnp.eye(S5_PW, dtype=F32)[None] * d.astype(F32).reshape(S5_PAIRS, 1, S5_PW)
    brow = brow + jnp.pad(skip, ((0, 0), (0, 0), (0, rows - S5_PW)))
    brow = brow.astype(BF16)
    wtoep = jnp.stack(
        [jnp.pad(brow, ((0, 0), (0, 0), (s * S5_PW, 0)))[:, :, :rows] for s in range(T)], axis=0)
    lam = jnp.stack([pw_re[:, T, 0], pw_im[:, T, 0]], axis=1)
    return wst.astype(BF16), wtoep, wo.astype(BF16), lam


def _s5_perm():
    n = S5_PPT * LANES
    a = np.arange(n)
    k, pp, j = a // LANES, (a % LANES) // S5_PW, a % S5_PW
    p = np.zeros((n, n), np.float32)
    p[a, pp * LANES + k * S5_PW + j] = 1.0
    return p


def _s5_kernel(u_ref, wst_ref, wt_ref, wo_ref, lam_ref, perm_ref, permt_ref, y_ref,
               x_s, uc_s, s_s, h_s, cre_s, cim_s, *, batch, tt):
    nch = tt // S5_CHUNK
    quads = S5_CHUNK * S5_PW // LANES
    per_q = LANES // S5_PW

    @pl.when(pl.program_id(1) == 0)
    def _():
        cre_s[...] = jnp.zeros_like(cre_s)
        cim_s[...] = jnp.zeros_like(cim_s)

    u2 = u_ref.at[0]
    y2 = y_ref.at[0]

    perm = perm_ref[...]
    for q in range(quads):
        xq = jnp.concatenate(
            [jnp.concatenate([u2[pl.ds(n * S5_CHUNK + q * per_q + k, batch, stride=tt), :]
                              for k in range(per_q)], axis=1) for n in range(nch)], axis=0)
        ucq = _dot(xq.astype(BF16), perm)
        for pp in range(S5_PPT):
            uc_s[pp, :, q * LANES:(q + 1) * LANES] = ucq[:, pp * LANES:(pp + 1) * LANES].astype(BF16)

    for pp in range(S5_PPT):
        u = uc_s[pp]
        s_s[...] = _dot(u, wst_ref[pp])
        lre = jnp.broadcast_to(lam_ref[pp, 0:1, :], (batch, LANES))
        lim = jnp.broadcast_to(lam_ref[pp, 1:2, :], (batch, LANES))

        def scan(n, carry):
            hr, hi = carry
            r0 = pl.multiple_of(n * batch, batch)
            h_s[pl.ds(r0, batch), 0:LANES] = hr
            h_s[pl.ds(r0, batch), LANES:2 * LANES] = hi
            sr = s_s[pl.ds(r0, batch), 0:LANES]
            si = s_s[pl.ds(r0, batch), LANES:2 * LANES]
            return (lre * hr - lim * hi + sr, lre * hi + lim * hr + si)

        hr, hi = lax.fori_loop(0, nch, scan, (cre_s[pp], cim_s[pp]))
        cre_s[pp] = hr
        cim_s[pp] = hi
        wt = wt_ref[:, pp].reshape(S5_CHUNK * S5_PW, S5_CHUNK * S5_PW)
        y = _dot(u, wt) + _dot_nt(h_s[...].astype(BF16), wo_ref[pp])
        for q in range(quads):
            x_s[q, :, pp * LANES:(pp + 1) * LANES] = y[:, q * LANES:(q + 1) * LANES]

    permt = permt_ref[...]
    for q in range(quads):
        yq = x_s[q]
        hi = yq.astype(BF16)
        lo = (yq - hi.astype(F32)).astype(BF16)
        yn = _dot(hi, permt) + _dot(lo, permt)
        for n in range(nch):
            for k in range(per_q):
                y2[pl.ds(n * S5_CHUNK + q * per_q + k, batch, stride=tt), :] = (
                    yn[n * batch:(n + 1) * batch, k * LANES:(k + 1) * LANES])


def _s5_scan(u4, weights, b, s):
    wst, wt, wo, lam = weights
    tt = min(S5_TT, s)
    rows = tt // S5_CHUNK * b
    width = S5_CHUNK * S5_PW
    perm = _s5_perm()
    act = pl.BlockSpec((1, b * tt, LANES), lambda c, j: (c, j, 0))
    wspec = lambda a_, b_: pl.BlockSpec((S5_PPT, a_, b_), lambda c, j: (c, 0, 0))
    return pl.pallas_call(
        functools.partial(_s5_kernel, batch=b, tt=tt),
        grid=(S5_TILES, s // tt),
        in_specs=[act, wspec(width, 2 * LANES),
                  pl.BlockSpec((S5_CHUNK, S5_PPT, S5_PW, width), lambda c, j: (0, c, 0, 0)),
                  wspec(width, 2 * LANES),
                  wspec(2, LANES), _const_spec((width, width)), _const_spec((width, width))],
        out_specs=act,
        out_shape=jax.ShapeDtypeStruct(u4.shape, F32),
        scratch_shapes=[pltpu.VMEM((S5_PPT, rows, width), F32),
                        pltpu.VMEM((S5_PPT, rows, width), BF16),
                        pltpu.VMEM((rows, 2 * LANES), F32), pltpu.VMEM((rows, 2 * LANES), F32),
                        pltpu.VMEM((S5_PPT, b, LANES), F32), pltpu.VMEM((S5_PPT, b, LANES), F32)],
        compiler_params=pltpu.CompilerParams(dimension_semantics=("parallel", "arbitrary"),
                                             vmem_limit_bytes=VMEM_LIMIT),
        name="s5_scan",
    )(u4, wst, wt, wo, lam, jnp.asarray(perm, BF16), jnp.asarray(perm.T, BF16))


def _attn_kernel(q_ref, k_ref, v_ref, o_ref,
                 kring, vring, sc0, sp0, sc1, sp1, m_s, l_s, n_s, *, tt):
    tile = pl.program_id(1)
    step = pl.program_id(2)
    ng = len(ATT_PAIRS)
    group = step % ng
    src = group * ATT_OUT_TILES + step // ng
    qv, kv, vv, ov = q_ref.at[0, 0], k_ref.at[0, 0], v_ref.at[0, 0], o_ref.at[0, 0]
    nblk = tt // ATT_BLOCK
    ring = 2 * nblk
    half = (tile % 2) * nblk

    @pl.when(tile == 0)
    def _():
        zero = jnp.zeros((nblk, ATT_BLOCK, LANES), BF16)
        kring[src, nblk:ring] = zero
        vring[src, nblk:ring] = zero

    lane_lo = lax.broadcasted_iota(jnp.int32, (1, LANES), 1) < ATT_HEAD_DIM
    row = lax.broadcasted_iota(jnp.int32, (2 * ATT_BLOCK, ATT_BLOCK), 0) % ATT_BLOCK
    col = lax.broadcasted_iota(jnp.int32, (2 * ATT_BLOCK, ATT_BLOCK), 1)
    mask_cur = col <= row
    mask_prev = col >= row
    neg = jnp.float32(-jnp.inf)

    def halves(t):
        return jnp.where(lane_lo, t[:ATT_BLOCK], t[ATT_BLOCK:])

    def rows_of(i, dil):
        return pl.ds((i // dil) * (ATT_BLOCK * dil) + i % dil, ATT_BLOCK, stride=dil)

    def stage_a(i, dil, sc_buf, sp_buf):
        rows = rows_of(i, dil)
        q = qv[rows, :]
        kc = kv[rows, :].astype(BF16)
        slot = half + i
        kring[src, slot] = kc
        vring[src, slot] = vv[rows, :].astype(BF16)
        kp = kring[src, (slot + ring - dil) % ring]
        has_prev = jnp.logical_or(tile > 0, i >= dil)
        q2 = jnp.concatenate([jnp.where(lane_lo, q, 0.0), jnp.where(lane_lo, 0.0, q)],
                             axis=0).astype(BF16)
        sc_buf[...] = jnp.where(mask_cur, _dot_nt(q2, kc), neg)
        sp_buf[...] = jnp.where(jnp.logical_and(mask_prev, has_prev), _dot_nt(q2, kp), neg)

    def stage_b(i, dil, gi, sc_buf, sp_buf):
        rows = rows_of(i, dil)
        slot = half + i
        sc, sp = sc_buf[...], sp_buf[...]
        m = jnp.max(jnp.maximum(sc, sp), axis=-1, keepdims=True)
        pc = jnp.exp2(sc - m)
        pp = jnp.exp2(sp - m)
        l = jnp.sum(pc + pp, axis=-1, keepdims=True)
        r = (_dot(pc.astype(BF16), vring[src, slot])
             + _dot(pp.astype(BF16), vring[src, (slot + ring - dil) % ring]))
        acc = halves(r)
        lb = halves(jnp.broadcast_to(l, (2 * ATT_BLOCK, LANES)))
        mb = halves(jnp.broadcast_to(m, (2 * ATT_BLOCK, LANES)))
        if gi > 0:
            m_old, l_old, n_old = m_s[rows, :], l_s[rows, :], n_s[rows, :]
            m_new = jnp.maximum(m_old, mb)
            a_old = jnp.exp2(m_old - m_new)
            a_new = jnp.exp2(mb - m_new)
            acc = n_old * a_old + acc * a_new
            lb = l_old * a_old + lb * a_new
            mb = m_new
        if gi == ng - 1:
            ov[rows, :] = acc / lb
        else:
            m_s[rows, :] = mb
            l_s[rows, :] = lb
            n_s[rows, :] = acc

    for gi, (_, dil) in enumerate(ATT_PAIRS):
        @pl.when(group == gi)
        def _(gi=gi, dil=dil):
            stage_a(0, dil, sc0, sp0)
            stage_a(1, dil, sc1, sp1)

            def body(ii, c):
                i0 = 2 * ii
                stage_b(i0, dil, gi, sc0, sp0)
                stage_b(i0 + 1, dil, gi, sc1, sp1)
                stage_a(jnp.minimum(i0 + 2, nblk - 1), dil, sc0, sp0)
                stage_a(jnp.minimum(i0 + 3, nblk - 1), dil, sc1, sp1)
                return c

            lax.fori_loop(0, nblk // 2, body, 0)


def _attention(q6, k6, v6):
    _, b, s, _ = q6.shape
    tt = min(ATT_TT, s)
    nblk = tt // ATT_BLOCK
    ng = len(ATT_PAIRS)
    src = lambda i: (i % ng) * ATT_OUT_TILES + i // ng
    blk = (1, 1, tt, LANES)
    cur = pl.BlockSpec(blk, lambda bi, t, i: (src(i), bi, t, 0))
    out = pl.BlockSpec(blk, lambda bi, t, i: (i // ng, bi, t, 0))
    score = pltpu.VMEM((2 * ATT_BLOCK, ATT_BLOCK), F32)
    return pl.pallas_call(
        functools.partial(_attn_kernel, tt=tt),
        grid=(b, s // tt, ATT_TILES),
        in_specs=[cur, cur, cur],
        out_specs=out,
        out_shape=jax.ShapeDtypeStruct((ATT_OUT_TILES, b, s, LANES), F32),
        scratch_shapes=[pltpu.VMEM((ATT_TILES, 2 * nblk, ATT_BLOCK, LANES), BF16)] * 2
        + [score] * 4 + [pltpu.VMEM((tt, LANES), F32)] * 3,
        compiler_params=pltpu.CompilerParams(
            dimension_semantics=("parallel", "arbitrary", "arbitrary"),
            vmem_limit_bytes=VMEM_LIMIT),
        name="attention",
    )(q6, k6, v6)


def _ssd_kernel(xd_ref, cw_ref, cb_ref, dtb_ref, alog_ref, dsk_ref, tril_ref, y_ref,
                xpad_s, state_s):
    @pl.when(pl.program_id(1) == 0)
    def _():
        xpad_s[0:8, :] = jnp.zeros((8, SSD_CONV_DIM), F32)
        state_s[...] = jnp.zeros_like(state_s)

    def chunk(ci, carry):
        rows = pl.ds(pl.multiple_of(ci * SSD_CHUNK, SSD_CHUNK), SSD_CHUNK)
        _ssd_chunk(rows, xd_ref, cw_ref, cb_ref, dtb_ref, alog_ref, dsk_ref, tril_ref, y_ref,
                   xpad_s, state_s)
        return carry

    lax.fori_loop(0, xd_ref.shape[1] // SSD_CHUNK, chunk, 0, unroll=2)


def _ssd_chunk(rows, xd_ref, cw_ref, cb_ref, dtb_ref, alog_ref, dsk_ref, tril_ref, y_ref,
               xpad_s, state_s):
    L = SSD_CHUNK
    xpad_s[8:8 + L, :] = xd_ref[0, rows, 0:SSD_CONV_DIM]
    xe = xpad_s[...]
    acc = cb_ref[...] + cw_ref[SSD_CONV - 1:SSD_CONV, :] * xe[8:8 + L]
    for back in range(1, SSD_CONV):
        kk = SSD_CONV - 1 - back
        acc = acc + cw_ref[kk:kk + 1, :] * pltpu.roll(xe, back, axis=0)[8:8 + L]
    xpad_s[0:8, :] = xe[L:L + 8]
    xc = _silu(acc)

    lane = lax.broadcasted_iota(jnp.int32, (1, LANES), 1)
    dtr = xd_ref[0, rows, SSD_CONV_DIM:SSD_IN] + dtb_ref[...]
    dt = jnp.maximum(dtr, 0.0) + jnp.log(1.0 + jnp.exp(-jnp.abs(dtr)))
    a = jnp.where(lane < SSD_HEADS, -jnp.exp(alog_ref[...]) * math.log2(math.e), 0.0)
    a_dt = dt * a
    a_cs = jnp.dot(tril_ref[...], a_dt, preferred_element_type=F32,
                   precision=lax.Precision.HIGHEST)
    ea = jnp.exp2(a_cs)
    nhp = 16
    last_t = a_cs.T[0:nhp, L - 1:L]
    r_t = (a_cs - jnp.log2(dt)).T[0:nhp]
    w_t = jnp.exp2(last_t - r_t)
    el_t = jnp.exp2(last_t)

    row = lax.broadcasted_iota(jnp.int32, (L, L), 0)
    col = lax.broadcasted_iota(jnp.int32, (L, L), 1)
    tri = row >= col
    lane_lo = lane < SSD_HEAD_DIM
    neg = jnp.float32(-jnp.inf)

    for g in range(SSD_GROUPS):
        bm = xc[:, SSD_WIDTH + g * SSD_STATE:SSD_WIDTH + (g + 1) * SSD_STATE]
        cm = xc[:, SSD_WIDTH + (SSD_GROUPS + g) * SSD_STATE:
                SSD_WIDTH + (SSD_GROUPS + g + 1) * SSD_STATE].astype(BF16)
        cb = _dot_nt(cm, bm.astype(BF16))
        bm_t = bm.T
        for pr in range(3 * g, 3 * g + 3):
            xs = xc[:, pr * LANES:(pr + 1) * LANES]
            x_lo = jnp.where(lane_lo, xs, 0.0).astype(BF16)
            x_hi = jnp.where(lane_lo, 0.0, xs).astype(BF16)
            mats, bws, cols, decs = [], [], [], []
            for h in (2 * pr, 2 * pr + 1):
                c_col = a_cs[:, h:h + 1]
                dec = jnp.exp2(jnp.where(tri, c_col - r_t[h:h + 1, :], neg))
                mats.append((cb * dec).astype(BF16))
                bws.append((bm_t * w_t[h:h + 1, :]).astype(BF16))
                cols.append(ea[:, h:h + 1])
                decs.append(el_t[h:h + 1, :])
            st = state_s[pr]
            y_diag = _dot(mats[0], x_lo) + _dot(mats[1], x_hi)
            y_off = _dot(cm, st.astype(BF16)) * jnp.where(lane_lo, cols[0], cols[1])
            y_ref[0, rows, pr * LANES:(pr + 1) * LANES] = (
                y_diag + y_off + xs * dsk_ref[:, pr * LANES:(pr + 1) * LANES])
            inc = _dot(bws[0], x_lo) + _dot(bws[1], x_hi)
            state_s[pr] = st * jnp.where(lane_lo, decs[0], decs[1]) + inc


def _ssd_scan(xd, conv_w, conv_b, dt_bias, a_log, d):
    b, s, _ = xd.shape
    tt = min(SSD_TT, s)
    pad = lambda t: jnp.pad(t.astype(F32), (0, SSD_DT_PAD - SSD_HEADS)).reshape(1, SSD_DT_PAD)
    dsk = jnp.repeat(d.astype(F32), SSD_HEAD_DIM).reshape(1, SSD_WIDTH)
    tril = jnp.asarray(np.tril(np.ones((SSD_CHUNK, SSD_CHUNK), np.float32)))
    return pl.pallas_call(
        _ssd_kernel,
        grid=(b, s // tt),
        in_specs=[pl.BlockSpec((1, tt, SSD_IN), lambda bi, c: (bi, c, 0)),
                  _const_spec((SSD_CONV, SSD_CONV_DIM)), _const_spec((1, SSD_CONV_DIM)),
                  _const_spec((1, SSD_DT_PAD)), _const_spec((1, SSD_DT_PAD)),
                  _const_spec((1, SSD_WIDTH)), _const_spec((SSD_CHUNK, SSD_CHUNK))],
        out_specs=pl.BlockSpec((1, tt, SSD_WIDTH), lambda bi, c: (bi, c, 0)),
        out_shape=jax.ShapeDtypeStruct((b, s, SSD_WIDTH), F32),
        scratch_shapes=[pltpu.VMEM((SSD_CHUNK + 8, SSD_CONV_DIM), F32),
                        pltpu.VMEM((SSD_HEADS // 2, SSD_STATE, LANES), F32)],
        compiler_params=pltpu.CompilerParams(dimension_semantics=("parallel", "arbitrary"),
                                             vmem_limit_bytes=VMEM_LIMIT),
        name="ssd_scan",
    )(xd, conv_w.astype(F32), conv_b.astype(F32).reshape(1, SSD_CONV_DIM),
      pad(dt_bias), pad(a_log), dsk, tril)


def _gelu_tanh(x):
    c = math.sqrt(2.0 / math.pi)
    return 0.5 * x * (1.0 + jnp.tanh(c * (x + 0.044715 * (x * x * x))))


def _merge_kernel(x_ref, nw_ref, wz_ref, ya_ref, yb_ref, yc_ref, gw_ref, gb_ref, snw_ref,
                  pa_ref, pb_ref, pc_ref, wo_ref, out_ref):
    x = x_ref[...]
    h = _rms_rows(x, nw_ref[...]).astype(BF16)

    def gate(i):
        lo = Z_GATE + i * D_MODEL
        return _sigmoid(_dot(h, wz_ref[:, lo:lo + D_MODEL]))

    g = _gelu_tanh(jnp.concatenate([ya_ref[c] for c in range(S5_TILES)], axis=1))
    y_a = g * _sigmoid(_dot(g.astype(BF16), gw_ref[...]) + gb_ref[...])
    y_a = y_a * _silu(_dot(h, wz_ref[:, Z_ZA:Z_ZB]))
    merged = gate(0) * _dot(y_a.astype(BF16), pa_ref[...])

    y_b = jnp.concatenate([yb_ref[c] for c in range(ATT_OUT_TILES)], axis=1)
    y_b = y_b * _silu(_dot(h, wz_ref[:, Z_ZB:Z_ZC]))
    merged = merged + gate(1) * _dot(y_b.astype(BF16), pb_ref[...])

    y_c = yc_ref[...] * _silu(_dot(h, wz_ref[:, Z_ZC:Z_GATE]))
    y_c = _rms_rows(y_c, snw_ref[...])
    merged = merged + gate(2) * _dot(y_c.astype(BF16), pc_ref[...])

    out_ref[...] = x + _dot(merged.astype(BF16), wo_ref[...])


def _merge(x2, norm_w, w_z, ya4, yb2, yc, glu_w, glu_b, ssd_nw, pa, pb, pc, wo, batch, seq):
    n = x2.shape[0]
    tm = ROW_TILE
    row = lambda w: pl.BlockSpec((tm, w), lambda i: (i, 0))
    tiled = lambda t: pl.BlockSpec((t, tm, LANES), lambda i: (0, i, 0))
    s5_tiled = pl.BlockSpec(
        (S5_TILES, tm, LANES),
        lambda i: (0, _time_major_block(i, batch, seq, tm, min(S5_TT, seq)), 0))
    return pl.pallas_call(
        _merge_kernel,
        grid=(n // tm,),
        in_specs=[row(D_MODEL), _const_spec((1, D_MODEL)), _const_spec((D_MODEL, Z_WIDTH)),
                  s5_tiled, tiled(ATT_OUT_TILES), row(SSD_WIDTH),
                  _const_spec((S5_WIDTH, S5_WIDTH)), _const_spec((1, S5_WIDTH)),
                  _const_spec((1, SSD_WIDTH)), _const_spec((S5_WIDTH, D_MODEL)),
                  _const_spec((ATT_GW, D_MODEL)), _const_spec((SSD_WIDTH, D_MODEL)),
                  _const_spec((D_MODEL, D_MODEL))],
        out_specs=row(D_MODEL),
        out_shape=jax.ShapeDtypeStruct((n, D_MODEL), F32),
        compiler_params=pltpu.CompilerParams(dimension_semantics=("parallel",),
                                             vmem_limit_bytes=VMEM_LIMIT),
        name="merge",
    )(x2, norm_w, w_z, ya4, yb2, yc, glu_w, glu_b, ssd_nw, pa, pb, pc, wo)


def _layer(x, norm_w, w_in, s5_a_re, s5_a_im, s5_log_step, s5_b_re, s5_b_im, s5_c_re,
           s5_c_im, s5_d, s5_glu_w, s5_glu_b, q_norm_w, k_norm_w, conv_w, conv_b,
           dt_bias, ssd_a_log, ssd_d, ssd_norm_w, proj_a, proj_b, proj_c, w_out):
    b, s, _ = x.shape
    n = b * s
    x2 = x.reshape(n, D_MODEL)
    nw = norm_w.astype(F32).reshape(1, D_MODEL)
    w_a = jnp.concatenate(
        [w_in[:, O_UA:O_ZA], w_in[:, O_Q:O_ZB], w_in[:, O_XBC:O_ZC],
         jnp.zeros((D_MODEL, SSD_DT_PAD - SSD_HEADS), BF16)], axis=1)
    w_z = jnp.concatenate(
        [w_in[:, O_ZA:O_Q], w_in[:, O_ZB:O_XBC], w_in[:, O_ZC:O_END]], axis=1)
    qw = jnp.tile(q_norm_w.astype(F32) * (ATT_HEAD_DIM ** -0.5 * math.log2(math.e)),
                  ATT_WIDTH // ATT_HEAD_DIM)
    kw = jnp.tile(k_norm_w.astype(F32), ATT_WIDTH // ATT_HEAD_DIM)
    blk = np.arange(ATT_GW) // ATT_HEAD_DIM
    ones = jnp.asarray((blk[:, None] == blk[None, :]).astype(np.float32) / ATT_HEAD_DIM, BF16)

    u4, q6, k6, v6, xd = _inproj(x2, nw, w_a, qw.reshape(1, -1), kw.reshape(1, -1), ones, b, s)

    s5w = _s5_weights(s5_a_re, s5_a_im, s5_log_step, s5_b_re, s5_b_im, s5_c_re, s5_c_im, s5_d)
    ya4 = _s5_scan(u4, s5w, b, s)

    tile4 = lambda t: t.reshape(ATT_TILES, b, s, LANES)
    yb2 = _attention(tile4(q6), tile4(k6), tile4(v6)).reshape(ATT_OUT_TILES, n, LANES)

    y_c = _ssd_scan(xd.reshape(b, s, SSD_IN), conv_w, conv_b, dt_bias, ssd_a_log, ssd_d)

    out = _merge(x2, nw, w_z, ya4, yb2, y_c.reshape(n, SSD_WIDTH),
                 s5_glu_w.astype(BF16), s5_glu_b.astype(F32).reshape(1, S5_WIDTH),
                 ssd_norm_w.astype(F32).reshape(1, SSD_WIDTH),
                 proj_a.astype(BF16), proj_b.astype(BF16), proj_c.astype(BF16),
                 w_out.astype(BF16), b, s)
    return out.reshape(b, s, D_MODEL)


def kernel(x, norm_w, w_in, s5_a_re, s5_a_im, s5_log_step, s5_b_re, s5_b_im, s5_c_re, s5_c_im,
           s5_d, s5_glu_w, s5_glu_b, q_norm_w, k_norm_w, conv_w, conv_b, dt_bias, ssd_a_log,
           ssd_d, ssd_norm_w, proj_a, proj_b, proj_c, w_out):
    w_in = w_in.astype(BF16)
    params = (norm_w, w_in, s5_a_re, s5_a_im, s5_log_step, s5_b_re, s5_b_im, s5_c_re, s5_c_im,
              s5_d, s5_glu_w, s5_glu_b, q_norm_w, k_norm_w, conv_w, conv_b, dt_bias, ssd_a_log,
              ssd_d, ssd_norm_w, proj_a, proj_b, proj_c, w_out)
    for i in range(norm_w.shape[0]):
        x = _layer(x, *(p[i] for p in params))
    return x
```

```python
import functools
import math

import numpy as np
import jax
import jax.numpy as jnp
from jax import lax
from jax.experimental import pallas as pl
from jax.experimental.pallas import tpu as pltpu

D_MODEL = 1024
RMS_EPS = 1e-6
LANES = 128

S5_WIDTH = 512
S5_GROUP = 16
S5_GROUPS = 32
S5_STATE = 64
S5_CHUNK = 16
S5_PAIRS = S5_GROUPS // 2
S5_TILES = S5_WIDTH // LANES
S5_PPT = S5_PAIRS // S5_TILES
S5_PW = 2 * S5_GROUP
S5_TT = 512

ATT_HEAD_DIM = 64
ATT_PAIRS = ((128, 1), (512, 4), (2048, 16))
ATT_HPG = 4
ATT_WIDTH = 768
ATT_GW = ATT_HPG * ATT_HEAD_DIM
ATT_BLOCK = 128
ATT_TILES = ATT_WIDTH // LANES
ATT_OUT_TILES = ATT_GW // LANES
ATT_TT = 2048

SSD_HEAD_DIM = 64
SSD_WIDTH = 768
SSD_HEADS = 12
SSD_GROUPS = 2
SSD_STATE = 128
SSD_CONV = 4
SSD_CHUNK = 128
SSD_CONV_DIM = 1280
SSD_DT_PAD = 128
SSD_IN = SSD_CONV_DIM + SSD_DT_PAD
SSD_TT = 512

IN_SPLITS = (512, 512, 768, 768, 768, 256, 1280, 12, 768, 3072)
_OFF = np.concatenate([[0], np.cumsum(IN_SPLITS)]).tolist()
(O_UA, O_ZA, O_Q, O_K, O_V, O_ZB, O_XBC, O_DT, O_ZC, O_GATE, O_END) = _OFF

A_UA, A_Q, A_K, A_V, A_XBC = 0, 512, 1280, 2048, 2816
A_WIDTH = A_XBC + SSD_IN
Z_ZA, Z_ZB, Z_ZC, Z_GATE = 0, 512, 768, 1536
Z_WIDTH = Z_GATE + 3 * D_MODEL

ROW_TILE = 512
VMEM_LIMIT = 56 * 1024 * 1024

BF16 = jnp.bfloat16
F32 = jnp.float32


def _dot(a, b):
    return jnp.dot(a, b, preferred_element_type=F32)


def _dot_nt(a, b):
    return lax.dot_general(a, b, (((1,), (1,)), ((), ())), preferred_element_type=F32)


def _const_spec(shape):
    nd = len(shape)
    return pl.BlockSpec(shape, lambda *_: (0,) * nd, pipeline_mode=pl.Buffered(1))


def _sigmoid(x):
    return 1.0 / (1.0 + jnp.exp(-x))


def _silu(x):
    return x * _sigmoid(x)


def _rms_rows(x, w):
    return x * lax.rsqrt(jnp.mean(x * x, axis=-1, keepdims=True) + RMS_EPS) * w


def _inproj_kernel(x_ref, nw_ref, w_ref, qw_ref, kw_ref, ones_ref,
                   ua_ref, q_ref, k_ref, v_ref, xd_ref):
    h = _rms_rows(x_ref[...], nw_ref[...]).astype(BF16)
    for c in range(0, S5_TILES, 2):
        t = _dot(h, w_ref[:, A_UA + c * LANES:A_UA + (c + 2) * LANES])
        ua_ref[c] = t[:, :LANES]
        ua_ref[c + 1] = t[:, LANES:]
    for col, ref in ((A_Q, q_ref), (A_K, k_ref), (A_V, v_ref)):
        for j in range(ATT_WIDTH // ATT_GW):
            t = _dot(h, w_ref[:, col + j * ATT_GW:col + (j + 1) * ATT_GW])
            ref[2 * j] = t[:, :LANES]
            ref[2 * j + 1] = t[:, LANES:]
    for lo in range(0, SSD_IN, 2 * LANES):
        hi = min(lo + 2 * LANES, SSD_IN)
        xd_ref[:, lo:hi] = _dot(h, w_ref[:, A_XBC + lo:A_XBC + hi])

    ones = ones_ref[...]
    for ref, gain_ref in ((q_ref, qw_ref), (k_ref, kw_ref)):
        for j in range(ATT_WIDTH // ATT_GW):
            t = jnp.concatenate([ref[2 * j], ref[2 * j + 1]], axis=1)
            ms = _dot((t * t).astype(BF16), ones)
            t = t * lax.rsqrt(ms + RMS_EPS) * gain_ref[:, j * ATT_GW:(j + 1) * ATT_GW]
            ref[2 * j] = t[:, :LANES]
            ref[2 * j + 1] = t[:, LANES:]


def _time_major_block(i, batch, seq, tm, tt):
    per_seq = seq // tm
    b, r = i // per_seq, i % per_seq
    sub = tt // tm
    return ((r // sub) * batch + b) * sub + r % sub


def _inproj(x2, norm_w, w_a, qw, kw, ones, batch, seq):
    n = x2.shape[0]
    tm = ROW_TILE
    row = lambda w: pl.BlockSpec((tm, w), lambda i: (i, 0))
    tiled = lambda t: pl.BlockSpec((t, tm, LANES), lambda i: (0, i, 0))
    s5_tiled = pl.BlockSpec(
        (S5_TILES, tm, LANES),
        lambda i: (0, _time_major_block(i, batch, seq, tm, min(S5_TT, seq)), 0))
    return pl.pallas_call(
        _inproj_kernel,
        grid=(n // tm,),
        in_specs=[row(D_MODEL), _const_spec((1, D_MODEL)), _const_spec((D_MODEL, A_WIDTH)),
                  _const_spec((1, ATT_WIDTH)), _const_spec((1, ATT_WIDTH)),
                  _const_spec((ATT_GW, ATT_GW))],
        out_specs=[s5_tiled, tiled(ATT_TILES), tiled(ATT_TILES), tiled(ATT_TILES), row(SSD_IN)],
        out_shape=[jax.ShapeDtypeStruct((S5_TILES, n, LANES), F32),
                   jax.ShapeDtypeStruct((ATT_TILES, n, LANES), F32),
                   jax.ShapeDtypeStruct((ATT_TILES, n, LANES), F32),
                   jax.ShapeDtypeStruct((ATT_TILES, n, LANES), F32),
                   jax.ShapeDtypeStruct((n, SSD_IN), F32)],
        compiler_params=pltpu.CompilerParams(dimension_semantics=("parallel",),
                                             vmem_limit_bytes=VMEM_LIMIT),
        name="inproj",
    )(x2, norm_w, w_a, qw, kw, ones)


def _s5_weights(a_re, a_im, log_step, b_re, b_im, c_re, c_im, d):
    G, P, I, T = S5_GROUPS, S5_STATE, S5_GROUP, S5_CHUNK
    a_re, a_im = a_re.astype(F32), a_im.astype(F32)
    b_re, b_im = b_re.astype(F32), b_im.astype(F32)
    c_re, c_im = c_re.astype(F32), c_im.astype(F32)
    step = jnp.exp(log_step.astype(F32))[:, None]
    mag = jnp.exp(a_re * step)
    ang = a_im * step
    lam_re, lam_im = mag * jnp.cos(ang), mag * jnp.sin(ang)
    num_re, num_im = lam_re - 1.0, lam_im
    den = a_re * a_re + a_im * a_im
    f_re = (num_re * a_re + num_im * a_im) / den
    f_im = (num_im * a_re - num_re * a_im) / den
    bb_re = f_re[..., None] * b_re - f_im[..., None] * b_im
    bb_im = f_re[..., None] * b_im + f_im[..., None] * b_re
    pw_re, pw_im = [jnp.ones_like(lam_re)], [jnp.zeros_like(lam_im)]
    for _ in range(T):
        r, i = pw_re[-1], pw_im[-1]
        pw_re.append(r * lam_re - i * lam_im)
        pw_im.append(r * lam_im + i * lam_re)
    pw_re, pw_im = jnp.stack(pw_re), jnp.stack(pw_im)
    hp = lax.Precision.HIGHEST
    rows = T * S5_PW
    pw_re = pw_re.reshape(T + 1, S5_PAIRS, 1, 2 * P).transpose(1, 0, 2, 3)
    pw_im = pw_im.reshape(T + 1, S5_PAIRS, 1, 2 * P).transpose(1, 0, 2, 3)

    def blockdiag(w):
        w = w.reshape((S5_PAIRS, 2) + w.shape[1:])
        z = jnp.zeros_like(w[:, 0])
        return jnp.concatenate([jnp.concatenate([w[:, 0], z], axis=-1),
                                jnp.concatenate([z, w[:, 1]], axis=-1)], axis=1)[:, None]

    bt_re, bt_im = blockdiag(bb_re.transpose(0, 2, 1)), blockdiag(bb_im.transpose(0, 2, 1))
    ct_re, ct_im = blockdiag(c_re), blockdiag(c_im)

    def scaled(w_re, w_im, q_re, q_im):
        re = (w_re * q_re - w_im * q_im).reshape(S5_PAIRS, -1, 2 * P)
        im = (w_re * q_im + w_im * q_re).reshape(S5_PAIRS, -1, 2 * P)
        return re, im

    wst_re, wst_im = scaled(bt_re, bt_im, pw_re[:, T - 1::-1][:, :T], pw_im[:, T - 1::-1][:, :T])
    wst = jnp.concatenate([wst_re, wst_im], axis=2)
    wo_re, wo_im = scaled(ct_re, ct_im, pw_re[:, 1:T + 1], pw_im[:, 1:T + 1])
    wo = jnp.concatenate([wo_re, -wo_im], axis=2)
    g_re, g_im = scaled(ct_re, ct_im, pw_re[:, 0:T], pw_im[:, 0:T])
    brow = (jnp.einsum('rap,rcp->rac', bt_re[:, 0], g_re, precision=hp)
            - jnp.einsum('rap,rcp->rac', bt_im[:, 0], g_im, precision=hp))
    skip = jnp.eye(S5_PW, dtype=F32)[None] * d.astype(F32).reshape(S5_PAIRS, 1, S5_PW)
    brow = brow + jnp.pad(skip, ((0, 0), (0, 0), (0, rows - S5_PW)))
    brow = brow.astype(BF16)
    wtoep = jnp.stack(
        [jnp.pad(brow, ((0, 0), (0, 0), (s * S5_PW, 0)))[:, :, :rows] for s in range(T)], axis=0)
    lam = jnp.stack([pw_re[:, T, 0], pw_im[:, T, 0]], axis=1)
    return wst.astype(BF16), wtoep, wo.astype(BF16), lam


def _s5_perm():
    n = S5_PPT * LANES
    a = np.arange(n)
    k, pp, j = a // LANES, (a % LANES) // S5_PW, a % S5_PW
    p = np.zeros((n, n), np.float32)
    p[a, pp * LANES + k * S5_PW + j] = 1.0
    return p


def _s5_kernel(u_ref, wst_ref, wt_ref, wo_ref, lam_ref, perm_ref, permt_ref, y_ref,
               x_s, uc_s, s_s, h_s, cre_s, cim_s, *, batch, tt):
    nch = tt // S5_CHUNK
    quads = S5_CHUNK * S5_PW // LANES
    per_q = LANES // S5_PW

    @pl.when(pl.program_id(1) == 0)
    def _():
        cre_s[...] = jnp.zeros_like(cre_s)
        cim_s[...] = jnp.zeros_like(cim_s)

    u2 = u_ref.at[0]
    y2 = y_ref.at[0]

    perm = perm_ref[...]
    for q in range(quads):
        xq = jnp.concatenate(
            [jnp.concatenate([u2[pl.ds(n * S5_CHUNK + q * per_q + k, batch, stride=tt), :]
                              for k in range(per_q)], axis=1) for n in range(nch)], axis=0)
        ucq = _dot(xq.astype(BF16), perm)
        for pp in range(S5_PPT):
            uc_s[pp, :, q * LANES:(q + 1) * LANES] = ucq[:, pp * LANES:(pp + 1) * LANES].astype(BF16)

    for pp in range(S5_PPT):
        u = uc_s[pp]
        s_s[...] = _dot(u, wst_ref[pp])
        lre = jnp.broadcast_to(lam_ref[pp, 0:1, :], (batch, LANES))
        lim = jnp.broadcast_to(lam_ref[pp, 1:2, :], (batch, LANES))

        def scan(n, carry):
            hr, hi = carry
            r0 = pl.multiple_of(n * batch, batch)
            h_s[pl.ds(r0, batch), 0:LANES] = hr
            h_s[pl.ds(r0, batch), LANES:2 * LANES] = hi
            sr = s_s[pl.ds(r0, batch), 0:LANES]
            si = s_s[pl.ds(r0, batch), LANES:2 * LANES]
            return (lre * hr - lim * hi + sr, lre * hi + lim * hr + si)

        hr, hi = lax.fori_loop(0, nch, scan, (cre_s[pp], cim_s[pp]))
        cre_s[pp] = hr
        cim_s[pp] = hi
        wt = wt_ref[:, pp].reshape(S5_CHUNK * S5_PW, S5_CHUNK * S5_PW)
        y = _dot(u, wt) + _dot_nt(h_s[...].astype(BF16), wo_ref[pp])
        for q in range(quads):
            x_s[q, :, pp * LANES:(pp + 1) * LANES] = y[:, q * LANES:(q + 1) * LANES]

    permt = permt_ref[...]
    for q in range(quads):
        yq = x_s[q]
        hi = yq.astype(BF16)
        lo = (yq - hi.astype(F32)).astype(BF16)
        yn = _dot(hi, permt) + _dot(lo, permt)
        for n in range(nch):
            for k in range(per_q):
                y2[pl.ds(n * S5_CHUNK + q * per_q + k, batch, stride=tt), :] = (
                    yn[n * batch:(n + 1) * batch, k * LANES:(k + 1) * LANES])


def _s5_scan(u4, weights, b, s):
    wst, wt, wo, lam = weights
    tt = min(S5_TT, s)
    rows = tt // S5_CHUNK * b
    width = S5_CHUNK * S5_PW
    perm = _s5_perm()
    act = pl.BlockSpec((1, b * tt, LANES), lambda c, j: (c, j, 0))
    wspec = lambda a_, b_: pl.BlockSpec((S5_PPT, a_, b_), lambda c, j: (c, 0, 0))
    return pl.pallas_call(
        functools.partial(_s5_kernel, batch=b, tt=tt),
        grid=(S5_TILES, s // tt),
        in_specs=[act, wspec(width, 2 * LANES),
                  pl.BlockSpec((S5_CHUNK, S5_PPT, S5_PW, width), lambda c, j: (0, c, 0, 0)),
                  wspec(width, 2 * LANES),
                  wspec(2, LANES), _const_spec((width, width)), _const_spec((width, width))],
        out_specs=act,
        out_shape=jax.ShapeDtypeStruct(u4.shape, F32),
        scratch_shapes=[pltpu.VMEM((S5_PPT, rows, width), F32),
                        pltpu.VMEM((S5_PPT, rows, width), BF16),
                        pltpu.VMEM((rows, 2 * LANES), F32), pltpu.VMEM((rows, 2 * LANES), F32),
                        pltpu.VMEM((S5_PPT, b, LANES), F32), pltpu.VMEM((S5_PPT, b, LANES), F32)],
        compiler_params=pltpu.CompilerParams(dimension_semantics=("parallel", "arbitrary"),
                                             vmem_limit_bytes=VMEM_LIMIT),
        name="s5_scan",
    )(u4, wst, wt, wo, lam, jnp.asarray(perm, BF16), jnp.asarray(perm.T, BF16))


def _attn_kernel(q_ref, k_ref, v_ref, o_ref,
                 kring, vring, sc0, sp0, sc1, sp1, m_s, l_s, n_s, *, tt):
    tile = pl.program_id(1)
    step = pl.program_id(2)
    ng = len(ATT_PAIRS)
    order = step % ng
    src = (ng - 1 - order) * ATT_OUT_TILES + step // ng
    qv, kv, vv, ov = q_ref.at[0, 0], k_ref.at[0, 0], v_ref.at[0, 0], o_ref.at[0, 0]
    nblk = tt // ATT_BLOCK
    ring = 2 * nblk
    half = (tile % 2) * nblk

    @pl.when(tile == 0)
    def _():
        zero = jnp.zeros((nblk, ATT_BLOCK, LANES), BF16)
        kring[src, nblk:ring] = zero
        vring[src, nblk:ring] = zero

    lane_lo = lax.broadcasted_iota(jnp.int32, (1, LANES), 1) < ATT_HEAD_DIM
    row = lax.broadcasted_iota(jnp.int32, (2 * ATT_BLOCK, ATT_BLOCK), 0) % ATT_BLOCK
    col = lax.broadcasted_iota(jnp.int32, (2 * ATT_BLOCK, ATT_BLOCK), 1)
    mask_cur = col <= row
    mask_prev = col >= row
    neg = jnp.float32(-jnp.inf)

    def halves(t):
        return jnp.where(lane_lo, t[:ATT_BLOCK], t[ATT_BLOCK:])

    def rows_of(i, dil):
        return pl.ds((i // dil) * (ATT_BLOCK * dil) + i % dil, ATT_BLOCK, stride=dil)

    def stage_a(i, dil, sc_buf, sp_buf):
        rows = rows_of(i, dil)
        q = qv[rows, :]
        kc = kv[rows, :].astype(BF16)
        slot = half + i
        kring[src, slot] = kc
        vring[src, slot] = vv[rows, :].astype(BF16)
        kp = kring[src, (slot + ring - dil) % ring]
        has_prev = jnp.logical_or(tile > 0, i >= dil)
        q2 = jnp.concatenate([jnp.where(lane_lo, q, 0.0), jnp.where(lane_lo, 0.0, q)],
                             axis=0).astype(BF16)
        sc_buf[...] = jnp.where(mask_cur, _dot_nt(q2, kc), neg)
        sp_buf[...] = jnp.where(jnp.logical_and(mask_prev, has_prev), _dot_nt(q2, kp), neg)

    def stage_b(i, dil, gi, sc_buf, sp_buf):
        rows = rows_of(i, dil)
        slot = half + i
        sc, sp = sc_buf[...], sp_buf[...]
        m = jnp.max(jnp.maximum(sc, sp), axis=-1, keepdims=True)
        pc = jnp.exp2(sc - m)
        pp = jnp.exp2(sp - m)
        l = jnp.sum(pc + pp, axis=-1, keepdims=True)
        r = (_dot(pc.astype(BF16), vring[src, slot])
             + _dot(pp.astype(BF16), vring[src, (slot + ring - dil) % ring]))
        acc = halves(r)
        lb = halves(jnp.broadcast_to(l, (2 * ATT_BLOCK, LANES)))
        mb = halves(jnp.broadcast_to(m, (2 * ATT_BLOCK, LANES)))
        if gi > 0:
            m_old, l_old, n_old = m_s[rows, :], l_s[rows, :], n_s[rows, :]
            m_new = jnp.maximum(m_old, mb)
            a_old = jnp.exp2(m_old - m_new)
            a_new = jnp.exp2(mb - m_new)
            acc = n_old * a_old + acc * a_new
            lb = l_old * a_old + lb * a_new
            mb = m_new
        if gi == ng - 1:
            ov[rows, :] = acc / lb
        else:
            m_s[rows, :] = mb
            l_s[rows, :] = lb
            n_s[rows, :] = acc

    for gi in range(ng):
        @pl.when(order == gi)
        def _(gi=gi, dil=ATT_PAIRS[ng - 1 - gi][1]):
            stage_a(0, dil, sc0, sp0)
            stage_a(1, dil, sc1, sp1)

            def body(ii, c):
                i0 = 2 * ii
                stage_b(i0, dil, gi, sc0, sp0)
                stage_b(i0 + 1, dil, gi, sc1, sp1)
                stage_a(jnp.minimum(i0 + 2, nblk - 1), dil, sc0, sp0)
                stage_a(jnp.minimum(i0 + 3, nblk - 1), dil, sc1, sp1)
                return c

            lax.fori_loop(0, nblk // 2, body, 0, unroll=2)


def _attention(q6, k6, v6):
    _, b, s, _ = q6.shape
    tt = min(ATT_TT, s)
    nblk = tt // ATT_BLOCK
    ng = len(ATT_PAIRS)
    src = lambda i: (ng - 1 - i % ng) * ATT_OUT_TILES + i // ng
    blk = (1, 1, tt, LANES)
    cur = pl.BlockSpec(blk, lambda bi, t, i: (src(i), bi, t, 0))
    out = pl.BlockSpec(blk, lambda bi, t, i: (i // ng, bi, t, 0))
    score = pltpu.VMEM((2 * ATT_BLOCK, ATT_BLOCK), F32)
    return pl.pallas_call(
        functools.partial(_attn_kernel, tt=tt),
        grid=(b, s // tt, ATT_TILES),
        in_specs=[cur, cur, cur],
        out_specs=out,
        out_shape=jax.ShapeDtypeStruct((ATT_OUT_TILES, b, s, LANES), F32),
        scratch_shapes=[pltpu.VMEM((ATT_TILES, 2 * nblk, ATT_BLOCK, LANES), BF16)] * 2
        + [score] * 4 + [pltpu.VMEM((tt, LANES), F32)] * 3,
        compiler_params=pltpu.CompilerParams(
            dimension_semantics=("parallel", "arbitrary", "arbitrary"),
            vmem_limit_bytes=VMEM_LIMIT),
        name="attention",
    )(q6, k6, v6)


def _ssd_kernel(xd_ref, cw_ref, cb_ref, dtb_ref, alog_ref, dsk_ref, tril_ref, y_ref,
                xpad_s, state_s):
    @pl.when(pl.program_id(1) == 0)
    def _():
        xpad_s[0:8, :] = jnp.zeros((8, SSD_CONV_DIM), F32)
        state_s[...] = jnp.zeros_like(state_s)

    def chunk(ci, carry):
        rows = pl.ds(pl.multiple_of(ci * SSD_CHUNK, SSD_CHUNK), SSD_CHUNK)
        _ssd_chunk(rows, xd_ref, cw_ref, cb_ref, dtb_ref, alog_ref, dsk_ref, tril_ref, y_ref,
                   xpad_s, state_s)
        return carry

    lax.fori_loop(0, xd_ref.shape[1] // SSD_CHUNK, chunk, 0, unroll=2)


def _ssd_chunk(rows, xd_ref, cw_ref, cb_ref, dtb_ref, alog_ref, dsk_ref, tril_ref, y_ref,
               xpad_s, state_s):
    L = SSD_CHUNK
    xpad_s[8:8 + L, :] = xd_ref[0, rows, 0:SSD_CONV_DIM]
    xe = xpad_s[...]
    acc = cb_ref[...] + cw_ref[SSD_CONV - 1:SSD_CONV, :] * xe[8:8 + L]
    for back in range(1, SSD_CONV):
        kk = SSD_CONV - 1 - back
        acc = acc + cw_ref[kk:kk + 1, :] * pltpu.roll(xe, back, axis=0)[8:8 + L]
    xpad_s[0:8, :] = xe[L:L + 8]
    xc = _silu(acc)

    lane = lax.broadcasted_iota(jnp.int32, (1, LANES), 1)
    dtr = xd_ref[0, rows, SSD_CONV_DIM:SSD_IN] + dtb_ref[...]
    dt = jnp.maximum(dtr, 0.0) + jnp.log(1.0 + jnp.exp(-jnp.abs(dtr)))
    a = jnp.where(lane < SSD_HEADS, -jnp.exp(alog_ref[...]) * math.log2(math.e), 0.0)
    a_dt = dt * a
    a_cs = jnp.dot(tril_ref[...], a_dt, preferred_element_type=F32,
                   precision=lax.Precision.HIGHEST)
    ea = jnp.exp2(a_cs)
    nhp = 16
    last_t = a_cs.T[0:nhp, L - 1:L]
    r_t = (a_cs - jnp.log2(dt)).T[0:nhp]
    w_t = jnp.exp2(last_t - r_t)
    el_t = jnp.exp2(last_t)

    row = lax.broadcasted_iota(jnp.int32, (L, L), 0)
    col = lax.broadcasted_iota(jnp.int32, (L, L), 1)
    tri = row >= col
    lane_lo = lane < SSD_HEAD_DIM
    neg = jnp.float32(-jnp.inf)

    for g in range(SSD_GROUPS):
        bm = xc[:, SSD_WIDTH + g * SSD_STATE:SSD_WIDTH + (g + 1) * SSD_STATE]
        cm = xc[:, SSD_WIDTH + (SSD_GROUPS + g) * SSD_STATE:
                SSD_WIDTH + (SSD_GROUPS + g + 1) * SSD_STATE].astype(BF16)
        cb = _dot_nt(cm, bm.astype(BF16))
        bm_t = bm.T
        for pr in range(3 * g, 3 * g + 3):
            xs = xc[:, pr * LANES:(pr + 1) * LANES]
            x_lo = jnp.where(lane_lo, xs, 0.0).astype(BF16)
            x_hi = jnp.where(lane_lo, 0.0, xs).astype(BF16)
            mats, bws, cols, decs = [], [], [], []
            for h in (2 * pr, 2 * pr + 1):
                c_col = a_cs[:, h:h + 1]
                dec = jnp.exp2(jnp.where(tri, c_col - r_t[h:h + 1, :], neg))
                mats.append((cb * dec).astype(BF16))
                bws.append((bm_t * w_t[h:h + 1, :]).astype(BF16))
                cols.append(ea[:, h:h + 1])
                decs.append(el_t[h:h + 1, :])
            st = state_s[pr]
            y_diag = _dot(mats[0], x_lo) + _dot(mats[1], x_hi)
            y_off = _dot(cm, st.astype(BF16)) * jnp.where(lane_lo, cols[0], cols[1])
            y_ref[0, rows, pr * LANES:(pr + 1) * LANES] = (
                y_diag + y_off + xs * dsk_ref[:, pr * LANES:(pr + 1) * LANES])
            inc = _dot(bws[0], x_lo) + _dot(bws[1], x_hi)
            state_s[pr] = st * jnp.where(lane_lo, decs[0], decs[1]) + inc


def _ssd_scan(xd, conv_w, conv_b, dt_bias, a_log, d):
    b, s, _ = xd.shape
    tt = min(SSD_TT, s)
    pad = lambda t: jnp.pad(t.astype(F32), (0, SSD_DT_PAD - SSD_HEADS)).reshape(1, SSD_DT_PAD)
    dsk = jnp.repeat(d.astype(F32), SSD_HEAD_DIM).reshape(1, SSD_WIDTH)
    tril = jnp.asarray(np.tril(np.ones((SSD_CHUNK, SSD_CHUNK), np.float32)))
    return pl.pallas_call(
        _ssd_kernel,
        grid=(b, s // tt),
        in_specs=[pl.BlockSpec((1, tt, SSD_IN), lambda bi, c: (bi, c, 0)),
                  _const_spec((SSD_CONV, SSD_CONV_DIM)), _const_spec((1, SSD_CONV_DIM)),
                  _const_spec((1, SSD_DT_PAD)), _const_spec((1, SSD_DT_PAD)),
                  _const_spec((1, SSD_WIDTH)), _const_spec((SSD_CHUNK, SSD_CHUNK))],
        out_specs=pl.BlockSpec((1, tt, SSD_WIDTH), lambda bi, c: (bi, c, 0)),
        out_shape=jax.ShapeDtypeStruct((b, s, SSD_WIDTH), F32),
        scratch_shapes=[pltpu.VMEM((SSD_CHUNK + 8, SSD_CONV_DIM), F32),
                        pltpu.VMEM((SSD_HEADS // 2, SSD_STATE, LANES), F32)],
        compiler_params=pltpu.CompilerParams(dimension_semantics=("parallel", "arbitrary"),
                                             vmem_limit_bytes=VMEM_LIMIT),
        name="ssd_scan",
    )(xd, conv_w.astype(F32), conv_b.astype(F32).reshape(1, SSD_CONV_DIM),
      pad(dt_bias), pad(a_log), dsk, tril)


def _gelu_tanh(x):
    c = math.sqrt(2.0 / math.pi)
    return 0.5 * x * (1.0 + jnp.tanh(c * (x + 0.044715 * (x * x * x))))


def _merge_kernel(x_ref, nw_ref, wz_ref, ya_ref, yb_ref, yc_ref, gw_ref, gb_ref, snw_ref,
                  pa_ref, pb_ref, pc_ref, wo_ref, out_ref):
    x = x_ref[...]
    h = _rms_rows(x, nw_ref[...]).astype(BF16)

    def gate(i):
        lo = Z_GATE + i * D_MODEL
        return _sigmoid(_dot(h, wz_ref[:, lo:lo + D_MODEL]))

    g = _gelu_tanh(jnp.concatenate([ya_ref[c] for c in range(S5_TILES)], axis=1))
    y_a = g * _sigmoid(_dot(g.astype(BF16), gw_ref[...]) + gb_ref[...])
    y_a = y_a * _silu(_dot(h, wz_ref[:, Z_ZA:Z_ZB]))
    merged = gate(0) * _dot(y_a.astype(BF16), pa_ref[...])

    y_b = jnp.concatenate([yb_ref[c] for c in range(ATT_OUT_TILES)], axis=1)
    y_b = y_b * _silu(_dot(h, wz_ref[:, Z_ZB:Z_ZC]))
    merged = merged + gate(1) * _dot(y_b.astype(BF16), pb_ref[...])

    y_c = yc_ref[...] * _silu(_dot(h, wz_ref[:, Z_ZC:Z_GATE]))
    y_c = _rms_rows(y_c, snw_ref[...])
    merged = merged + gate(2) * _dot(y_c.astype(BF16), pc_ref[...])

    out_ref[...] = x + _dot(merged.astype(BF16), wo_ref[...])


def _merge(x2, norm_w, w_z, ya4, yb2, yc, glu_w, glu_b, ssd_nw, pa, pb, pc, wo, batch, seq):
    n = x2.shape[0]
    tm = ROW_TILE
    row = lambda w: pl.BlockSpec((tm, w), lambda i: (i, 0))
    tiled = lambda t: pl.BlockSpec((t, tm, LANES), lambda i: (0, i, 0))
    s5_tiled = pl.BlockSpec(
        (S5_TILES, tm, LANES),
        lambda i: (0, _time_major_block(i, batch, seq, tm, min(S5_TT, seq)), 0))
    return pl.pallas_call(
        _merge_kernel,
        grid=(n // tm,),
        in_specs=[row(D_MODEL), _const_spec((1, D_MODEL)), _const_spec((D_MODEL, Z_WIDTH)),
                  s5_tiled, tiled(ATT_OUT_TILES), row(SSD_WIDTH),
                  _const_spec((S5_WIDTH, S5_WIDTH)), _const_spec((1, S5_WIDTH)),
                  _const_spec((1, SSD_WIDTH)), _const_spec((S5_WIDTH, D_MODEL)),
                  _const_spec((ATT_GW, D_MODEL)), _const_spec((SSD_WIDTH, D_MODEL)),
                  _const_spec((D_MODEL, D_MODEL))],
        out_specs=row(D_MODEL),
        out_shape=jax.ShapeDtypeStruct((n, D_MODEL), F32),
        compiler_params=pltpu.CompilerParams(dimension_semantics=("parallel",),
                                             vmem_limit_bytes=VMEM_LIMIT),
        name="merge",
    )(x2, norm_w, w_z, ya4, yb2, yc, glu_w, glu_b, ssd_nw, pa, pb, pc, wo)


def _layer(x, norm_w, w_in, s5_a_re, s5_a_im, s5_log_step, s5_b_re, s5_b_im, s5_c_re,
           s5_c_im, s5_d, s5_glu_w, s5_glu_b, q_norm_w, k_norm_w, conv_w, conv_b,
           dt_bias, ssd_a_log, ssd_d, ssd_norm_w, proj_a, proj_b, proj_c, w_out):
    b, s, _ = x.shape
    n = b * s
    x2 = x.reshape(n, D_MODEL)
    nw = norm_w.astype(F32).reshape(1, D_MODEL)
    w_a = jnp.concatenate(
        [w_in[:, O_UA:O_ZA], w_in[:, O_Q:O_ZB], w_in[:, O_XBC:O_ZC],
         jnp.zeros((D_MODEL, SSD_DT_PAD - SSD_HEADS), BF16)], axis=1)
    w_z = jnp.concatenate(
        [w_in[:, O_ZA:O_Q], w_in[:, O_ZB:O_XBC], w_in[:, O_ZC:O_END]], axis=1)
    qw = jnp.tile(q_norm_w.astype(F32) * (ATT_HEAD_DIM ** -0.5 * math.log2(math.e)),
                  ATT_WIDTH // ATT_HEAD_DIM)
    kw = jnp.tile(k_norm_w.astype(F32), ATT_WIDTH // ATT_HEAD_DIM)
    blk = np.arange(ATT_GW) // ATT_HEAD_DIM
    ones = jnp.asarray((blk[:, None] == blk[None, :]).astype(np.float32) / ATT_HEAD_DIM, BF16)

    u4, q6, k6, v6, xd = _inproj(x2, nw, w_a, qw.reshape(1, -1), kw.reshape(1, -1), ones, b, s)

    s5w = _s5_weights(s5_a_re, s5_a_im, s5_log_step, s5_b_re, s5_b_im, s5_c_re, s5_c_im, s5_d)
    ya4 = _s5_scan(u4, s5w, b, s)

    tile4 = lambda t: t.reshape(ATT_TILES, b, s, LANES)
    yb2 = _attention(tile4(q6), tile4(k6), tile4(v6)).reshape(ATT_OUT_TILES, n, LANES)

    y_c = _ssd_scan(xd.reshape(b, s, SSD_IN), conv_w, conv_b, dt_bias, ssd_a_log, ssd_d)

    out = _merge(x2, nw, w_z, ya4, yb2, y_c.reshape(n, SSD_WIDTH),
                 s5_glu_w.astype(BF16), s5_glu_b.astype(F32).reshape(1, S5_WIDTH),
                 ssd_norm_w.astype(F32).reshape(1, SSD_WIDTH),
                 proj_a.astype(BF16), proj_b.astype(BF16), proj_c.astype(BF16),
                 w_out.astype(BF16), b, s)
    return out.reshape(b, s, D_MODEL)


def kernel(x, norm_w, w_in, s5_a_re, s5_a_im, s5_log_step, s5_b_re, s5_b_im, s5_c_re, s5_c_im,
           s5_d, s5_glu_w, s5_glu_b, q_norm_w, k_norm_w, conv_w, conv_b, dt_bias, ssd_a_log,
           ssd_d, ssd_norm_w, proj_a, proj_b, proj_c, w_out):
    w_in = w_in.astype(BF16)
    params = (norm_w, w_in, s5_a_re, s5_a_im, s5_log_step, s5_b_re, s5_b_im, s5_c_re, s5_c_im,
              s5_d, s5_glu_w, s5_glu_b, q_norm_w, k_norm_w, conv_w, conv_b, dt_bias, ssd_a_log,
              ssd_d, ssd_norm_w, proj_a, proj_b, proj_c, w_out)
    for i in range(norm_w.shape[0]):
        x = _layer(x, *(p[i] for p in params))
    return x
```

```python
import functools
import math

import numpy as np
import jax
import jax.numpy as jnp
from jax import lax
from jax.experimental import pallas as pl
from jax.experimental.pallas import tpu as pltpu

D_MODEL = 1024
RMS_EPS = 1e-6
LANES = 128

S5_WIDTH = 512
S5_GROUP = 16
S5_GROUPS = 32
S5_STATE = 64
S5_CHUNK = 16
S5_PAIRS = S5_GROUPS // 2
S5_TILES = S5_WIDTH // LANES
S5_PPT = S5_PAIRS // S5_TILES
S5_PW = 2 * S5_GROUP
S5_TT = 1024

ATT_HEAD_DIM = 64
ATT_PAIRS = ((128, 1), (512, 4), (2048, 16))
ATT_HPG = 4
ATT_WIDTH = 768
ATT_GW = ATT_HPG * ATT_HEAD_DIM
ATT_BLOCK = 128
ATT_TILES = ATT_WIDTH // LANES
ATT_OUT_TILES = ATT_GW // LANES
ATT_TT = 2048

SSD_HEAD_DIM = 64
SSD_WIDTH = 768
SSD_HEADS = 12
SSD_GROUPS = 2
SSD_STATE = 128
SSD_CONV = 4
SSD_CHUNK = 128
SSD_CONV_DIM = 1280
SSD_DT_PAD = 128
SSD_IN = SSD_CONV_DIM + SSD_DT_PAD
SSD_TT = 512

IN_SPLITS = (512, 512, 768, 768, 768, 256, 1280, 12, 768, 3072)
_OFF = np.concatenate([[0], np.cumsum(IN_SPLITS)]).tolist()
(O_UA, O_ZA, O_Q, O_K, O_V, O_ZB, O_XBC, O_DT, O_ZC, O_GATE, O_END) = _OFF

A_UA, A_Q, A_K, A_V, A_XBC = 0, 512, 1280, 2048, 2816
A_WIDTH = A_XBC + SSD_IN
Z_ZA, Z_ZB, Z_ZC, Z_GATE = 0, 512, 768, 1536
Z_WIDTH = Z_GATE + 3 * D_MODEL

ROW_TILE = 512
VMEM_LIMIT = 56 * 1024 * 1024

BF16 = jnp.bfloat16
F32 = jnp.float32


def _dot(a, b):
    return jnp.dot(a, b, preferred_element_type=F32)


def _dot_nt(a, b):
    return lax.dot_general(a, b, (((1,), (1,)), ((), ())), preferred_element_type=F32)


def _const_spec(shape):
    nd = len(shape)
    return pl.BlockSpec(shape, lambda *_: (0,) * nd, pipeline_mode=pl.Buffered(1))


def _sigmoid(x):
    return 1.0 / (1.0 + jnp.exp(-x))


def _silu(x):
    return x * _sigmoid(x)


def _rms_rows(x, w):
    return x * lax.rsqrt(jnp.mean(x * x, axis=-1, keepdims=True) + RMS_EPS) * w


def _inproj_kernel(x_ref, nw_ref, w_ref, qw_ref, kw_ref, ones_ref,
                   ua_ref, q_ref, k_ref, v_ref, xd_ref):
    h = _rms_rows(x_ref[...], nw_ref[...]).astype(BF16)
    for c in range(0, S5_TILES, 2):
        t = _dot(h, w_ref[:, A_UA + c * LANES:A_UA + (c + 2) * LANES])
        ua_ref[c] = t[:, :LANES]
        ua_ref[c + 1] = t[:, LANES:]
    for col, ref in ((A_Q, q_ref), (A_K, k_ref), (A_V, v_ref)):
        for j in range(ATT_WIDTH // ATT_GW):
            t = _dot(h, w_ref[:, col + j * ATT_GW:col + (j + 1) * ATT_GW])
            ref[2 * j] = t[:, :LANES]
            ref[2 * j + 1] = t[:, LANES:]
    for lo in range(0, SSD_IN, 2 * LANES):
        hi = min(lo + 2 * LANES, SSD_IN)
        xd_ref[:, lo:hi] = _dot(h, w_ref[:, A_XBC + lo:A_XBC + hi])

    ones = ones_ref[...]
    for ref, gain_ref in ((q_ref, qw_ref), (k_ref, kw_ref)):
        for j in range(ATT_WIDTH // ATT_GW):
            t = jnp.concatenate([ref[2 * j], ref[2 * j + 1]], axis=1)
            ms = _dot((t * t).astype(BF16), ones)
            t = t * lax.rsqrt(ms + RMS_EPS) * gain_ref[:, j * ATT_GW:(j + 1) * ATT_GW]
            ref[2 * j] = t[:, :LANES]
            ref[2 * j + 1] = t[:, LANES:]


def _time_major_block(i, batch, seq, tm, tt):
    per_seq = seq // tm
    b, r = i // per_seq, i % per_seq
    sub = tt // tm
    return ((r // sub) * batch + b) * sub + r % sub


def _inproj(x2, norm_w, w_a, qw, kw, ones, batch, seq):
    n = x2.shape[0]
    tm = ROW_TILE
    row = lambda w: pl.BlockSpec((tm, w), lambda i: (i, 0))
    tiled = lambda t: pl.BlockSpec((t, tm, LANES), lambda i: (0, i, 0))
    s5_tiled = pl.BlockSpec(
        (S5_TILES, tm, LANES),
        lambda i: (0, _time_major_block(i, batch, seq, tm, min(S5_TT, seq)), 0))
    return pl.pallas_call(
        _inproj_kernel,
        grid=(n // tm,),
        in_specs=[row(D_MODEL), _const_spec((1, D_MODEL)), _const_spec((D_MODEL, A_WIDTH)),
                  _const_spec((1, ATT_WIDTH)), _const_spec((1, ATT_WIDTH)),
                  _const_spec((ATT_GW, ATT_GW))],
        out_specs=[s5_tiled, tiled(ATT_TILES), tiled(ATT_TILES), tiled(ATT_TILES), row(SSD_IN)],
        out_shape=[jax.ShapeDtypeStruct((S5_TILES, n, LANES), F32),
                   jax.ShapeDtypeStruct((ATT_TILES, n, LANES), F32),
                   jax.ShapeDtypeStruct((ATT_TILES, n, LANES), F32),
                   jax.ShapeDtypeStruct((ATT_TILES, n, LANES), F32),
                   jax.ShapeDtypeStruct((n, SSD_IN), F32)],
        compiler_params=pltpu.CompilerParams(dimension_semantics=("parallel",),
                                             vmem_limit_bytes=VMEM_LIMIT),
        name="inproj",
    )(x2, norm_w, w_a, qw, kw, ones)


def _s5_weights(a_re, a_im, log_step, b_re, b_im, c_re, c_im, d):
    G, P, I, T = S5_GROUPS, S5_STATE, S5_GROUP, S5_CHUNK
    a_re, a_im = a_re.astype(F32), a_im.astype(F32)
    b_re, b_im = b_re.astype(F32), b_im.astype(F32)
    c_re, c_im = c_re.astype(F32), c_im.astype(F32)
    step = jnp.exp(log_step.astype(F32))[:, None]
    mag = jnp.exp(a_re * step)
    ang = a_im * step
    lam_re, lam_im = mag * jnp.cos(ang), mag * jnp.sin(ang)
    num_re, num_im = lam_re - 1.0, lam_im
    den = a_re * a_re + a_im * a_im
    f_re = (num_re * a_re + num_im * a_im) / den
    f_im = (num_im * a_re - num_re * a_im) / den
    bb_re = f_re[..., None] * b_re - f_im[..., None] * b_im
    bb_im = f_re[..., None] * b_im + f_im[..., None] * b_re
    pw_re, pw_im = [jnp.ones_like(lam_re)], [jnp.zeros_like(lam_im)]
    for _ in range(T):
        r, i = pw_re[-1], pw_im[-1]
        pw_re.append(r * lam_re - i * lam_im)
        pw_im.append(r * lam_im + i * lam_re)
    pw_re, pw_im = jnp.stack(pw_re), jnp.stack(pw_im)
    hp = lax.Precision.HIGHEST
    rows = T * S5_PW
    pw_re = pw_re.reshape(T + 1, S5_PAIRS, 1, 2 * P).transpose(1, 0, 2, 3)
    pw_im = pw_im.reshape(T + 1, S5_PAIRS, 1, 2 * P).transpose(1, 0, 2, 3)

    def blockdiag(w):
        w = w.reshape((S5_PAIRS, 2) + w.shape[1:])
        z = jnp.zeros_like(w[:, 0])
        return jnp.concatenate([jnp.concatenate([w[:, 0], z], axis=-1),
                                jnp.concatenate([z, w[:, 1]], axis=-1)], axis=1)[:, None]

    bt_re, bt_im = blockdiag(bb_re.transpose(0, 2, 1)), blockdiag(bb_im.transpose(0, 2, 1))
    ct_re, ct_im = blockdiag(c_re), blockdiag(c_im)

    def scaled(w_re, w_im, q_re, q_im):
        re = (w_re * q_re - w_im * q_im).reshape(S5_PAIRS, -1, 2 * P)
        im = (w_re * q_im + w_im * q_re).reshape(S5_PAIRS, -1, 2 * P)
        return re, im

    wst_re, wst_im = scaled(bt_re, bt_im, pw_re[:, T - 1::-1][:, :T], pw_im[:, T - 1::-1][:, :T])
    wst = jnp.concatenate([wst_re, wst_im], axis=2)
    wo_re, wo_im = scaled(ct_re, ct_im, pw_re[:, 1:T + 1], pw_im[:, 1:T + 1])
    wo = jnp.concatenate([wo_re, -wo_im], axis=2)
    g_re, g_im = scaled(ct_re, ct_im, pw_re[:, 0:T], pw_im[:, 0:T])
    brow = (jnp.einsum('rap,rcp->rac', bt_re[:, 0], g_re, precision=hp)
            - jnp.einsum('rap,rcp->rac', bt_im[:, 0], g_im, precision=hp))
    skip = jnp.eye(S5_PW, dtype=F32)[None] * d.astype(F32).reshape(S5_PAIRS, 1, S5_PW)
    brow = brow + jnp.pad(skip, ((0, 0), (0, 0), (0, rows - S5_PW)))
    brow = brow.astype(BF16)
    wtoep = jnp.stack(
        [jnp.pad(brow, ((0, 0), (0, 0), (s * S5_PW, 0)))[:, :, :rows] for s in range(T)], axis=0)
    lam = jnp.stack([pw_re[:, T, 0], pw_im[:, T, 0]], axis=1)
    return wst.astype(BF16), wtoep, wo.astype(BF16), lam


def _s5_perm():
    n = S5_PPT * LANES
    a = np.arange(n)
    k, pp, j = a // LANES, (a % LANES) // S5_PW, a % S5_PW
    p = np.zeros((n, n), np.float32)
    p[a, pp * LANES + k * S5_PW + j] = 1.0
    return p


def _s5_kernel(u_ref, wst_ref, wt_ref, wo_ref, lam_ref, perm_ref, permt_ref, y_ref,
               x_s, uc_s, s_s, h_s, cre_s, cim_s, *, batch, tt):
    nch = tt // S5_CHUNK
    quads = S5_CHUNK * S5_PW // LANES
    per_q = LANES // S5_PW

    @pl.when(pl.program_id(1) == 0)
    def _():
        cre_s[...] = jnp.zeros_like(cre_s)
        cim_s[...] = jnp.zeros_like(cim_s)

    u2 = u_ref.at[0]
    y2 = y_ref.at[0]

    perm = perm_ref[...]
    for q in range(quads):
        xq = jnp.concatenate(
            [jnp.concatenate([u2[pl.ds(n * S5_CHUNK + q * per_q + k, batch, stride=tt), :]
                              for k in range(per_q)], axis=1) for n in range(nch)], axis=0)
        ucq = _dot(xq.astype(BF16), perm)
        for pp in range(S5_PPT):
            uc_s[pp, :, q * LANES:(q + 1) * LANES] = ucq[:, pp * LANES:(pp + 1) * LANES].astype(BF16)

    for pp in range(S5_PPT):
        u = uc_s[pp]
        s_s[...] = _dot(u, wst_ref[pp])
        lre = jnp.broadcast_to(lam_ref[pp, 0:1, :], (batch, LANES))
        lim = jnp.broadcast_to(lam_ref[pp, 1:2, :], (batch, LANES))

        def scan(n, carry):
            hr, hi = carry
            r0 = pl.multiple_of(n * batch, batch)
            h_s[pl.ds(r0, batch), 0:LANES] = hr
            h_s[pl.ds(r0, batch), LANES:2 * LANES] = hi
            sr = s_s[pl.ds(r0, batch), 0:LANES]
            si = s_s[pl.ds(r0, batch), LANES:2 * LANES]
            return (lre * hr - lim * hi + sr, lre * hi + lim * hr + si)

        hr, hi = lax.fori_loop(0, nch, scan, (cre_s[pp], cim_s[pp]))
        cre_s[pp] = hr
        cim_s[pp] = hi
        wt = wt_ref[:, pp].reshape(S5_CHUNK * S5_PW, S5_CHUNK * S5_PW)
        y = _dot(u, wt) + _dot_nt(h_s[...].astype(BF16), wo_ref[pp])
        for q in range(quads):
            x_s[q, :, pp * LANES:(pp + 1) * LANES] = y[:, q * LANES:(q + 1) * LANES]

    permt = permt_ref[...]
    for q in range(quads):
        yq = x_s[q]
        hi = yq.astype(BF16)
        lo = (yq - hi.astype(F32)).astype(BF16)
        yn = _dot(hi, permt) + _dot(lo, permt)
        for n in range(nch):
            for k in range(per_q):
                y2[pl.ds(n * S5_CHUNK + q * per_q + k, batch, stride=tt), :] = (
                    yn[n * batch:(n + 1) * batch, k * LANES:(k + 1) * LANES])


def _s5_scan(u4, weights, b, s):
    wst, wt, wo, lam = weights
    tt = min(S5_TT, s)
    rows = tt // S5_CHUNK * b
    width = S5_CHUNK * S5_PW
    perm = _s5_perm()
    act = pl.BlockSpec((1, b * tt, LANES), lambda c, j: (c, j, 0))
    wspec = lambda a_, b_: pl.BlockSpec((S5_PPT, a_, b_), lambda c, j: (c, 0, 0))
    return pl.pallas_call(
        functools.partial(_s5_kernel, batch=b, tt=tt),
        grid=(S5_TILES, s // tt),
        in_specs=[act, wspec(width, 2 * LANES),
                  pl.BlockSpec((S5_CHUNK, S5_PPT, S5_PW, width), lambda c, j: (0, c, 0, 0)),
                  wspec(width, 2 * LANES),
                  wspec(2, LANES), _const_spec((width, width)), _const_spec((width, width))],
        out_specs=act,
        out_shape=jax.ShapeDtypeStruct(u4.shape, F32),
        scratch_shapes=[pltpu.VMEM((S5_PPT, rows, width), F32),
                        pltpu.VMEM((S5_PPT, rows, width), BF16),
                        pltpu.VMEM((rows, 2 * LANES), F32), pltpu.VMEM((rows, 2 * LANES), F32),
                        pltpu.VMEM((S5_PPT, b, LANES), F32), pltpu.VMEM((S5_PPT, b, LANES), F32)],
        compiler_params=pltpu.CompilerParams(dimension_semantics=("parallel", "arbitrary"),
                                             vmem_limit_bytes=VMEM_LIMIT),
        name="s5_scan",
    )(u4, wst, wt, wo, lam, jnp.asarray(perm, BF16), jnp.asarray(perm.T, BF16))


def _attn_kernel(q_ref, k_ref, v_ref, o_ref,
                 kring, vring, sc0, sp0, sc1, sp1, m_s, l_s, n_s, *, tt):
    tile = pl.program_id(1)
    step = pl.program_id(2)
    ng = len(ATT_PAIRS)
    order = step % ng
    src = (ng - 1 - order) * ATT_OUT_TILES + step // ng
    qv, kv, vv, ov = q_ref.at[0, 0], k_ref.at[0, 0], v_ref.at[0, 0], o_ref.at[0, 0]
    nblk = tt // ATT_BLOCK
    ring = 2 * nblk
    half = (tile % 2) * nblk

    @pl.when(tile == 0)
    def _():
        zero = jnp.zeros((nblk, ATT_BLOCK, LANES), BF16)
        kring[src, nblk:ring] = zero
        vring[src, nblk:ring] = zero

    lane_lo = lax.broadcasted_iota(jnp.int32, (1, LANES), 1) < ATT_HEAD_DIM
    row = lax.broadcasted_iota(jnp.int32, (2 * ATT_BLOCK, ATT_BLOCK), 0) % ATT_BLOCK
    col = lax.broadcasted_iota(jnp.int32, (2 * ATT_BLOCK, ATT_BLOCK), 1)
    mask_cur = col <= row
    mask_prev = col >= row
    neg = jnp.float32(-jnp.inf)

    def halves(t):
        return jnp.where(lane_lo, t[:ATT_BLOCK], t[ATT_BLOCK:])

    def rows_of(i, dil):
        return pl.ds((i // dil) * (ATT_BLOCK * dil) + i % dil, ATT_BLOCK, stride=dil)

    def stage_a(i, dil, sc_buf, sp_buf):
        rows = rows_of(i, dil)
        q = qv[rows, :]
        kc = kv[rows, :].astype(BF16)
        slot = half + i
        kring[src, slot] = kc
        vring[src, slot] = vv[rows, :].astype(BF16)
        kp = kring[src, (slot + ring - dil) % ring]
        has_prev = jnp.logical_or(tile > 0, i >= dil)
        q2 = jnp.concatenate([jnp.where(lane_lo, q, 0.0), jnp.where(lane_lo, 0.0, q)],
                             axis=0).astype(BF16)
        sc_buf[...] = jnp.where(mask_cur, _dot_nt(q2, kc), neg)
        sp_buf[...] = jnp.where(jnp.logical_and(mask_prev, has_prev), _dot_nt(q2, kp), neg)

    def stage_b(i, dil, gi, sc_buf, sp_buf):
        rows = rows_of(i, dil)
        slot = half + i
        sc, sp = sc_buf[...], sp_buf[...]
        m = jnp.max(jnp.maximum(sc, sp), axis=-1, keepdims=True)
        pc = jnp.exp2(sc - m)
        pp = jnp.exp2(sp - m)
        l = jnp.sum(pc + pp, axis=-1, keepdims=True)
        r = (_dot(pc.astype(BF16), vring[src, slot])
             + _dot(pp.astype(BF16), vring[src, (slot + ring - dil) % ring]))
        acc = halves(r)
        lb = halves(jnp.broadcast_to(l, (2 * ATT_BLOCK, LANES)))
        mb = halves(jnp.broadcast_to(m, (2 * ATT_BLOCK, LANES)))
        if gi > 0:
            m_old, l_old, n_old = m_s[rows, :], l_s[rows, :], n_s[rows, :]
            m_new = jnp.maximum(m_old, mb)
            a_old = jnp.exp2(m_old - m_new)
            a_new = jnp.exp2(mb - m_new)
            acc = n_old * a_old + acc * a_new
            lb = l_old * a_old + lb * a_new
            mb = m_new
        if gi == ng - 1:
            ov[rows, :] = acc / lb
        else:
            m_s[rows, :] = mb
            l_s[rows, :] = lb
            n_s[rows, :] = acc

    for gi in range(ng):
        @pl.when(order == gi)
        def _(gi=gi, dil=ATT_PAIRS[ng - 1 - gi][1]):
            stage_a(0, dil, sc0, sp0)
            stage_a(1, dil, sc1, sp1)

            def body(ii, c):
                i0 = 2 * ii
                stage_b(i0, dil, gi, sc0, sp0)
                stage_b(i0 + 1, dil, gi, sc1, sp1)
                stage_a(jnp.minimum(i0 + 2, nblk - 1), dil, sc0, sp0)
                stage_a(jnp.minimum(i0 + 3, nblk - 1), dil, sc1, sp1)
                return c

            lax.fori_loop(0, nblk // 2, body, 0, unroll=2)


def _attention(q6, k6, v6):
    _, b, s, _ = q6.shape
    tt = min(ATT_TT, s)
    nblk = tt // ATT_BLOCK
    ng = len(ATT_PAIRS)
    src = lambda i: (ng - 1 - i % ng) * ATT_OUT_TILES + i // ng
    blk = (1, 1, tt, LANES)
    cur = pl.BlockSpec(blk, lambda bi, t, i: (src(i), bi, t, 0))
    out = pl.BlockSpec(blk, lambda bi, t, i: (i // ng, bi, t, 0))
    score = pltpu.VMEM((2 * ATT_BLOCK, ATT_BLOCK), F32)
    return pl.pallas_call(
        functools.partial(_attn_kernel, tt=tt),
        grid=(b, s // tt, ATT_TILES),
        in_specs=[cur, cur, cur],
        out_specs=out,
        out_shape=jax.ShapeDtypeStruct((ATT_OUT_TILES, b, s, LANES), F32),
        scratch_shapes=[pltpu.VMEM((ATT_TILES, 2 * nblk, ATT_BLOCK, LANES), BF16)] * 2
        + [score] * 4 + [pltpu.VMEM((tt, LANES), F32)] * 3,
        compiler_params=pltpu.CompilerParams(
            dimension_semantics=("parallel", "arbitrary", "arbitrary"),
            vmem_limit_bytes=VMEM_LIMIT),
        name="attention",
    )(q6, k6, v6)


def _ssd_kernel(xd_ref, cw_ref, cb_ref, dtb_ref, alog_ref, dsk_ref, tril_ref, y_ref,
                xpad_s, state_s):
    @pl.when(pl.program_id(1) == 0)
    def _():
        xpad_s[0:8, :] = jnp.zeros((8, SSD_CONV_DIM), F32)
        state_s[...] = jnp.zeros_like(state_s)

    def chunk(ci, carry):
        rows = pl.ds(pl.multiple_of(ci * SSD_CHUNK, SSD_CHUNK), SSD_CHUNK)
        _ssd_chunk(rows, xd_ref, cw_ref, cb_ref, dtb_ref, alog_ref, dsk_ref, tril_ref, y_ref,
                   xpad_s, state_s)
        return carry

    lax.fori_loop(0, xd_ref.shape[1] // SSD_CHUNK, chunk, 0, unroll=2)


def _ssd_chunk(rows, xd_ref, cw_ref, cb_ref, dtb_ref, alog_ref, dsk_ref, tril_ref, y_ref,
               xpad_s, state_s):
    L = SSD_CHUNK
    xpad_s[8:8 + L, :] = xd_ref[0, rows, 0:SSD_CONV_DIM]
    xe = xpad_s[...]
    acc = cb_ref[...] + cw_ref[SSD_CONV - 1:SSD_CONV, :] * xe[8:8 + L]
    for back in range(1, SSD_CONV):
        kk = SSD_CONV - 1 - back
        acc = acc + cw_ref[kk:kk + 1, :] * pltpu.roll(xe, back, axis=0)[8:8 + L]
    xpad_s[0:8, :] = xe[L:L + 8]
    xc = _silu(acc)

    lane = lax.broadcasted_iota(jnp.int32, (1, LANES), 1)
    dtr = xd_ref[0, rows, SSD_CONV_DIM:SSD_IN] + dtb_ref[...]
    dt = jnp.maximum(dtr, 0.0) + jnp.log(1.0 + jnp.exp(-jnp.abs(dtr)))
    a = jnp.where(lane < SSD_HEADS, -jnp.exp(alog_ref[...]) * math.log2(math.e), 0.0)
    a_dt = dt * a
    a_cs = jnp.dot(tril_ref[...], a_dt, preferred_element_type=F32,
                   precision=lax.Precision.HIGHEST)
    ea = jnp.exp2(a_cs)
    nhp = 16
    last_t = a_cs.T[0:nhp, L - 1:L]
    r_t = (a_cs - jnp.log2(dt)).T[0:nhp]
    w_t = jnp.exp2(last_t - r_t)
    el_t = jnp.exp2(last_t)

    row = lax.broadcasted_iota(jnp.int32, (L, L), 0)
    col = lax.broadcasted_iota(jnp.int32, (L, L), 1)
    tri = row >= col
    lane_lo = lane < SSD_HEAD_DIM
    neg = jnp.float32(-jnp.inf)

    for g in range(SSD_GROUPS):
        bm = xc[:, SSD_WIDTH + g * SSD_STATE:SSD_WIDTH + (g + 1) * SSD_STATE]
        cm = xc[:, SSD_WIDTH + (SSD_GROUPS + g) * SSD_STATE:
                SSD_WIDTH + (SSD_GROUPS + g + 1) * SSD_STATE].astype(BF16)
        cb = _dot_nt(cm, bm.astype(BF16))
        bm_t = bm.T
        for pr in range(3 * g, 3 * g + 3):
            xs = xc[:, pr * LANES:(pr + 1) * LANES]
            x_lo = jnp.where(lane_lo, xs, 0.0).astype(BF16)
            x_hi = jnp.where(lane_lo, 0.0, xs).astype(BF16)
            mats, bws, cols, decs = [], [], [], []
            for h in (2 * pr, 2 * pr + 1):
                c_col = a_cs[:, h:h + 1]
                dec = jnp.exp2(jnp.where(tri, c_col - r_t[h:h + 1, :], neg))
                mats.append((cb * dec).astype(BF16))
                bws.append((bm_t * w_t[h:h + 1, :]).astype(BF16))
                cols.append(ea[:, h:h + 1])
                decs.append(el_t[h:h + 1, :])
            st = state_s[pr]
            y_diag = _dot(mats[0], x_lo) + _dot(mats[1], x_hi)
            y_off = _dot(cm, st.astype(BF16)) * jnp.where(lane_lo, cols[0], cols[1])
            y_ref[0, rows, pr * LANES:(pr + 1) * LANES] = (
                y_diag + y_off + xs * dsk_ref[:, pr * LANES:(pr + 1) * LANES])
            inc = _dot(bws[0], x_lo) + _dot(bws[1], x_hi)
            state_s[pr] = st * jnp.where(lane_lo, decs[0], decs[1]) + inc


def _ssd_scan(xd, conv_w, conv_b, dt_bias, a_log, d):
    b, s, _ = xd.shape
    tt = min(SSD_TT, s)
    pad = lambda t: jnp.pad(t.astype(F32), (0, SSD_DT_PAD - SSD_HEADS)).reshape(1, SSD_DT_PAD)
    dsk = jnp.repeat(d.astype(F32), SSD_HEAD_DIM).reshape(1, SSD_WIDTH)
    tril = jnp.asarray(np.tril(np.ones((SSD_CHUNK, SSD_CHUNK), np.float32)))
    return pl.pallas_call(
        _ssd_kernel,
        grid=(b, s // tt),
        in_specs=[pl.BlockSpec((1, tt, SSD_IN), lambda bi, c: (bi, c, 0)),
                  _const_spec((SSD_CONV, SSD_CONV_DIM)), _const_spec((1, SSD_CONV_DIM)),
                  _const_spec((1, SSD_DT_PAD)), _const_spec((1, SSD_DT_PAD)),
                  _const_spec((1, SSD_WIDTH)), _const_spec((SSD_CHUNK, SSD_CHUNK))],
        out_specs=pl.BlockSpec((1, tt, SSD_WIDTH), lambda bi, c: (bi, c, 0)),
        out_shape=jax.ShapeDtypeStruct((b, s, SSD_WIDTH), F32),
        scratch_shapes=[pltpu.VMEM((SSD_CHUNK + 8, SSD_CONV_DIM), F32),
                        pltpu.VMEM((SSD_HEADS // 2, SSD_STATE, LANES), F32)],
        compiler_params=pltpu.CompilerParams(dimension_semantics=("parallel", "arbitrary"),
                                             vmem_limit_bytes=VMEM_LIMIT),
        name="ssd_scan",
    )(xd, conv_w.astype(F32), conv_b.astype(F32).reshape(1, SSD_CONV_DIM),
      pad(dt_bias), pad(a_log), dsk, tril)


def _gelu_tanh(x):
    c = math.sqrt(2.0 / math.pi)
    return 0.5 * x * (1.0 + jnp.tanh(c * (x + 0.044715 * (x * x * x))))


def _merge_kernel(x_ref, nw_ref, wz_ref, ya_ref, yb_ref, yc_ref, gw_ref, gb_ref, snw_ref,
                  pa_ref, pb_ref, pc_ref, wo_ref, out_ref):
    x = x_ref[...]
    h = _rms_rows(x, nw_ref[...]).astype(BF16)

    def gate(i):
        lo = Z_GATE + i * D_MODEL
        return _sigmoid(_dot(h, wz_ref[:, lo:lo + D_MODEL]))

    g = _gelu_tanh(jnp.concatenate([ya_ref[c] for c in range(S5_TILES)], axis=1))
    y_a = g * _sigmoid(_dot(g.astype(BF16), gw_ref[...]) + gb_ref[...])
    y_a = y_a * _silu(_dot(h, wz_ref[:, Z_ZA:Z_ZB]))
    merged = gate(0) * _dot(y_a.astype(BF16), pa_ref[...])

    y_b = jnp.concatenate([yb_ref[c] for c in range(ATT_OUT_TILES)], axis=1)
    y_b = y_b * _silu(_dot(h, wz_ref[:, Z_ZB:Z_ZC]))
    merged = merged + gate(1) * _dot(y_b.astype(BF16), pb_ref[...])

    y_c = yc_ref[...] * _silu(_dot(h, wz_ref[:, Z_ZC:Z_GATE]))
    y_c = _rms_rows(y_c, snw_ref[...])
    merged = merged + gate(2) * _dot(y_c.astype(BF16), pc_ref[...])

    out_ref[...] = x + _dot(merged.astype(BF16), wo_ref[...])


def _merge(x2, norm_w, w_z, ya4, yb2, yc, glu_w, glu_b, ssd_nw, pa, pb, pc, wo, batch, seq):
    n = x2.shape[0]
    tm = ROW_TILE
    row = lambda w: pl.BlockSpec((tm, w), lambda i: (i, 0))
    tiled = lambda t: pl.BlockSpec((t, tm, LANES), lambda i: (0, i, 0))
    s5_tiled = pl.BlockSpec(
        (S5_TILES, tm, LANES),
        lambda i: (0, _time_major_block(i, batch, seq, tm, min(S5_TT, seq)), 0))
    return pl.pallas_call(
        _merge_kernel,
        grid=(n // tm,),
        in_specs=[row(D_MODEL), _const_spec((1, D_MODEL)), _const_spec((D_MODEL, Z_WIDTH)),
                  s5_tiled, tiled(ATT_OUT_TILES), row(SSD_WIDTH),
                  _const_spec((S5_WIDTH, S5_WIDTH)), _const_spec((1, S5_WIDTH)),
                  _const_spec((1, SSD_WIDTH)), _const_spec((S5_WIDTH, D_MODEL)),
                  _const_spec((ATT_GW, D_MODEL)), _const_spec((SSD_WIDTH, D_MODEL)),
                  _const_spec((D_MODEL, D_MODEL))],
        out_specs=row(D_MODEL),
        out_shape=jax.ShapeDtypeStruct((n, D_MODEL), F32),
        compiler_params=pltpu.CompilerParams(dimension_semantics=("parallel",),
                                             vmem_limit_bytes=VMEM_LIMIT),
        name="merge",
    )(x2, norm_w, w_z, ya4, yb2, yc, glu_w, glu_b, ssd_nw, pa, pb, pc, wo)


def _layer(x, norm_w, w_in, s5_a_re, s5_a_im, s5_log_step, s5_b_re, s5_b_im, s5_c_re,
           s5_c_im, s5_d, s5_glu_w, s5_glu_b, q_norm_w, k_norm_w, conv_w, conv_b,
           dt_bias, ssd_a_log, ssd_d, ssd_norm_w, proj_a, proj_b, proj_c, w_out):
    b, s, _ = x.shape
    n = b * s
    x2 = x.reshape(n, D_MODEL)
    nw = norm_w.astype(F32).reshape(1, D_MODEL)
    w_a = jnp.concatenate(
        [w_in[:, O_UA:O_ZA], w_in[:, O_Q:O_ZB], w_in[:, O_XBC:O_ZC],
         jnp.zeros((D_MODEL, SSD_DT_PAD - SSD_HEADS), BF16)], axis=1)
    w_z = jnp.concatenate(
        [w_in[:, O_ZA:O_Q], w_in[:, O_ZB:O_XBC], w_in[:, O_ZC:O_END]], axis=1)
    qw = jnp.tile(q_norm_w.astype(F32) * (ATT_HEAD_DIM ** -0.5 * math.log2(math.e)),
                  ATT_WIDTH // ATT_HEAD_DIM)
    kw = jnp.tile(k_norm_w.astype(F32), ATT_WIDTH // ATT_HEAD_DIM)
    blk = np.arange(ATT_GW) // ATT_HEAD_DIM
    ones = jnp.asarray((blk[:, None] == blk[None, :]).astype(np.float32) / ATT_HEAD_DIM, BF16)

    u4, q6, k6, v6, xd = _inproj(x2, nw, w_a, qw.reshape(1, -1), kw.reshape(1, -1), ones, b, s)

    s5w = _s5_weights(s5_a_re, s5_a_im, s5_log_step, s5_b_re, s5_b_im, s5_c_re, s5_c_im, s5_d)
    ya4 = _s5_scan(u4, s5w, b, s)

    tile4 = lambda t: t.reshape(ATT_TILES, b, s, LANES)
    yb2 = _attention(tile4(q6), tile4(k6), tile4(v6)).reshape(ATT_OUT_TILES, n, LANES)

    y_c = _ssd_scan(xd.reshape(b, s, SSD_IN), conv_w, conv_b, dt_bias, ssd_a_log, ssd_d)

    out = _merge(x2, nw, w_z, ya4, yb2, y_c.reshape(n, SSD_WIDTH),
                 s5_glu_w.astype(BF16), s5_glu_b.astype(F32).reshape(1, S5_WIDTH),
                 ssd_norm_w.astype(F32).reshape(1, SSD_WIDTH),
                 proj_a.astype(BF16), proj_b.astype(BF16), proj_c.astype(BF16),
                 w_out.astype(BF16), b, s)
    return out.reshape(b, s, D_MODEL)


def kernel(x, norm_w, w_in, s5_a_re, s5_a_im, s5_log_step, s5_b_re, s5_b_im, s5_c_re, s5_c_im,
           s5_d, s5_glu_w, s5_glu_b, q_norm_w, k_norm_w, conv_w, conv_b, dt_bias, ssd_a_log,
           ssd_d, ssd_norm_w, proj_a, proj_b, proj_c, w_out):
    w_in = w_in.astype(BF16)
    params = (norm_w, w_in, s5_a_re, s5_a_im, s5_log_step, s5_b_re, s5_b_im, s5_c_re, s5_c_im,
              s5_d, s5_glu_w, s5_glu_b, q_norm_w, k_norm_w, conv_w, conv_b, dt_bias, ssd_a_log,
              ssd_d, ssd_norm_w, proj_a, proj_b, proj_c, w_out)
    for i in range(norm_w.shape[0]):
        x = _layer(x, *(p[i] for p in params))
    return x
```
